```python
import math
import jax
import jax.numpy as jnp
from jax import lax
import numpy as np

D_MODEL = 1024
BATCH = 32
SEQ = 2048
DEPTH = 4
DEC_BATCH = 8
DEC_SEQ = 32
PAST_LEN = 1024

CHUNK = 64
N_MIXERS = 4
N_SB = (DEPTH + 3) // N_MIXERS
N_S5 = (DEPTH + 2) // N_MIXERS
N_GDN = (DEPTH + 1) // N_MIXERS
N_CNV = DEPTH // N_MIXERS

SB_HEADS = 16
SB_HEAD_DIM = D_MODEL // SB_HEADS
SB_BLOCK = 128

S5_GROUP = 16
S5_GROUPS = D_MODEL // S5_GROUP
S5_STATE = 64

GDN_HEADS = 8
GDN_HEAD_DIM = D_MODEL // GDN_HEADS
GDN_CONV = 4

CNV_WIDTH = 31
D_FF = 4 * D_MODEL
EPS = 1e-6
F32 = jnp.float32

kernel_name = 'hybrid_streaming_encoder_step'


def rms_norm(x, g):
    xf = x.astype(F32)
    y = xf * lax.rsqrt(jnp.mean(xf * xf, axis=-1, keepdims=True) + EPS)
    return (y * g.astype(F32)).astype(x.dtype)


def layer_norm(x, g, b):
    xf = x.astype(F32)
    xc = xf - jnp.mean(xf, axis=-1, keepdims=True)
    var = jnp.mean(xc * xc, axis=-1, keepdims=True)
    return (xc * lax.rsqrt(var + EPS) * g.astype(F32) + b.astype(F32)).astype(x.dtype)


def l2_normalize(x):
    return x * lax.rsqrt(jnp.sum(x * x, axis=-1, keepdims=True) + EPS)


def causal_dwconv(xc, w):
    return lax.conv_general_dilated(
        xc, w.astype(xc.dtype)[:, None, :], window_strides=(1,), padding='VALID',
        dimension_numbers=('NWC', 'WIO', 'NWC'), feature_group_count=xc.shape[-1])


def sb_block(q, k, v, q_pos, k_pos):
    z = jnp.einsum('bqhd,bkhd->bhqk', q.astype(F32), k.astype(F32)) * (SB_HEAD_DIM ** -0.5)
    mask = k_pos[None, :] < q_pos[:, None]
    log_1mb = jnp.where(mask, jax.nn.log_sigmoid(-z), 0.0)
    after = lax.cumsum(log_1mb, axis=3, reverse=True) - log_1mb
    w = jnp.exp(jnp.where(mask, jax.nn.log_sigmoid(z) + after, -jnp.inf))
    return jnp.einsum('bhqk,bkhd->bqhd', w, v.astype(F32))


def sb_mixer(h, k_past, v_past, w_qkv, w_o):
    B, T, _ = h.shape
    P = k_past.shape[1]
    qkv = (h @ w_qkv).reshape(B, T, 3, SB_HEADS, SB_HEAD_DIM)
    q, k, v = qkv[:, :, 0], qkv[:, :, 1], qkv[:, :, 2]
    k_all = jnp.concatenate([k_past.astype(k.dtype), k], axis=1)
    v_all = jnp.concatenate([v_past.astype(v.dtype), v], axis=1)
    blk = min(SB_BLOCK, T)
    outs = []
    for lo in range(0, T, blk):
        hi = min(lo + blk, T)
        n_keys = P + hi
        outs.append(sb_block(q[:, lo:hi], k_all[:, :n_keys], v_all[:, :n_keys],
                             P + jnp.arange(lo, hi), jnp.arange(n_keys)))
    o = jnp.concatenate(outs, axis=1).reshape(B, T, D_MODEL).astype(h.dtype)
    return o @ w_o, k, v


def _cplx_scan_op(left, right):
    ar1, ai1, br1, bi1 = left
    ar2, ai2, br2, bi2 = right
    return (ar2 * ar1 - ai2 * ai1, ar2 * ai1 + ai2 * ar1,
            ar2 * br1 - ai2 * bi1 + br2, ar2 * bi1 + ai2 * br1 + bi2)


def s5_mixer(h, s_re, s_im, a_re, a_im, log_dt, b_re, b_im, c_re, c_im, d_skip, w_glu):
    B, T, _ = h.shape
    dt = jnp.exp(log_dt.astype(F32))[:, None]
    ar, ai = a_re.astype(F32), a_im.astype(F32)
    mag = jnp.exp(ar * dt)
    abar_re, abar_im = mag * jnp.cos(ai * dt), mag * jnp.sin(ai * dt)
    den = ar * ar + ai * ai
    f_re = ((abar_re - 1.0) * ar + abar_im * ai) / den
    f_im = (abar_im * ar - (abar_re - 1.0) * ai) / den
    br, bi = b_re.astype(F32), b_im.astype(F32)
    bb_re = f_re[..., None] * br - f_im[..., None] * bi
    bb_im = f_re[..., None] * bi + f_im[..., None] * br
    cr, ci = c_re.astype(F32), c_im.astype(F32)
    u = h.astype(F32)
    c = min(CHUNK, T)
    u_blocks = u.reshape(B, T // c, c, S5_GROUPS, S5_GROUP).transpose(1, 0, 2, 3, 4)

    def step(carry, u_blk):
        h_re, h_im = carry
        x_re = jnp.einsum('bcgn,gpn->bcgp', u_blk, bb_re)
        x_im = jnp.einsum('bcgn,gpn->bcgp', u_blk, bb_im)
        x_re = x_re.at[:, 0].add(abar_re * h_re - abar_im * h_im)
        x_im = x_im.at[:, 0].add(abar_re * h_im + abar_im * h_re)
        a_r = jnp.broadcast_to(abar_re, x_re.shape)
        a_i = jnp.broadcast_to(abar_im, x_im.shape)
        _, _, st_re, st_im = lax.associative_scan(_cplx_scan_op, (a_r, a_i, x_re, x_im), axis=1)
        y = jnp.einsum('bcgp,gnp->bcgn', st_re, cr) - jnp.einsum('bcgp,gnp->bcgn', st_im, ci)
        return (st_re[:, -1], st_im[:, -1]), y

    (new_re, new_im), y = lax.scan(step, (s_re.astype(F32), s_im.astype(F32)), u_blocks)
    y = y.transpose(1, 0, 2, 3, 4).reshape(B, T, D_MODEL) + d_skip.astype(F32) * u
    y = jax.nn.gelu(y).astype(h.dtype)
    g = y @ w_glu
    return g[..., :D_MODEL] * jax.nn.sigmoid(g[..., D_MODEL:]), new_re, new_im


def gated_delta_chunked(q, k, v, g, beta, s0):
    B, T, H, dk = q.shape
    dv = v.shape[-1]
    c = min(CHUNK, T)
    n = T // c

    def blocks(x):
        return x.reshape(B, n, c, H, x.shape[-1]).transpose(1, 0, 3, 2, 4)

    qc, kc, vc = blocks(q), blocks(k), blocks(v)
    gc = g.reshape(B, n, c, H).transpose(1, 0, 3, 2)
    bc = beta.reshape(B, n, c, H).transpose(1, 0, 3, 2)
    G = jnp.cumsum(gc, axis=-1)
    incl = jnp.tril(jnp.ones((c, c), dtype=bool))
    strict = jnp.tril(jnp.ones((c, c), dtype=bool), -1)
    decay = jnp.exp(jnp.where(incl, G[..., :, None] - G[..., None, :], -jnp.inf))
    kk = jnp.einsum('nbhid,nbhjd->nbhij', kc, kc)
    m = jnp.eye(c, dtype=F32) + jnp.where(strict, bc[..., :, None] * kk * decay, 0.0)
    rhs = jnp.concatenate([bc[..., None] * vc, (bc * jnp.exp(G))[..., None] * kc], axis=-1)
    sol = lax.linalg.triangular_solve(m, rhs, left_side=True, lower=True, unit_diagonal=True)
    u, w = sol[..., :dv], sol[..., dv:]
    qk = jnp.einsum('nbhid,nbhjd->nbhij', qc, kc) * decay

    def step(s, xs):
        q_i, k_i, u_i, w_i, G_i, qk_i = xs
        v_new = u_i - jnp.einsum('bhck,bhkv->bhcv', w_i, s)
        o = (jnp.einsum('bhck,bhkv->bhcv', q_i * jnp.exp(G_i)[..., None], s)
             + jnp.einsum('bhij,bhjv->bhiv', qk_i, v_new))
        G_last = G_i[..., -1:]
        s = (s * jnp.exp(G_last)[..., None]
             + jnp.einsum('bhck,bhcv->bhkv', k_i * jnp.exp(G_last - G_i)[..., None], v_new))
        return s, o

    s_fin, o = lax.scan(step, s0, (qc, kc, u, w, G, qk))
    return o.transpose(1, 0, 3, 2, 4).reshape(B, T, H, dv), s_fin


def gdn_mixer(h, s0, cbuf, w_in, conv_w, a_log, dt_bias, norm_g, w_o):
    B, T, _ = h.shape
    H, dh = GDN_HEADS, GDN_HEAD_DIM
    proj = h @ w_in
    qkv = proj[..., :3 * D_MODEL]
    z = proj[..., 3 * D_MODEL:4 * D_MODEL].astype(F32).reshape(B, T, H, dh)
    a = proj[..., 4 * D_MODEL:4 * D_MODEL + H].astype(F32)
    b = proj[..., 4 * D_MODEL + H:].astype(F32)
    xc = jnp.concatenate([cbuf.astype(qkv.dtype), qkv], axis=1)
    new_buf = xc[:, xc.shape[1] - (GDN_CONV - 1):]
    qkv = jax.nn.silu(causal_dwconv(xc, conv_w).astype(F32)).reshape(B, T, 3, H, dh)
    q = l2_normalize(qkv[:, :, 0]) * (dh ** -0.5)
    k = l2_normalize(qkv[:, :, 1])
    v = qkv[:, :, 2]
    beta = jax.nn.sigmoid(b)
    g = -jnp.exp(a_log.astype(F32)) * jax.nn.softplus(a + dt_bias.astype(F32))
    o, s_new = gated_delta_chunked(q, k, v, g, beta, s0.astype(F32))
    o = (o * lax.rsqrt(jnp.mean(o * o, axis=-1, keepdims=True) + EPS)
         * norm_g.astype(F32) * jax.nn.silu(z))
    return o.reshape(B, T, D_MODEL).astype(h.dtype) @ w_o, s_new, new_buf


def conv_mixer(h, cbuf, w_pw1, dw_w, dw_b, ln_g, ln_b, w_pw2):
    g = h @ w_pw1
    u = g[..., :D_MODEL] * jax.nn.sigmoid(g[..., D_MODEL:])
    uc = jnp.concatenate([cbuf.astype(u.dtype), u], axis=1)
    new_buf = uc[:, uc.shape[1] - (CNV_WIDTH - 1):]
    y = causal_dwconv(uc, dw_w) + dw_b.astype(u.dtype)
    y = jax.nn.silu(layer_norm(y, ln_g, ln_b))
    return y @ w_pw2, new_buf


def sq_relu_mlp(h, w1, w2):
    a = jax.nn.relu(h @ w1)
    return (a * a) @ w2


def trunk(x, c, sb_k, sb_v, s5_re, s5_im, gdn_s, gdn_cb, cnv_cb, weights):
    (ada_w, ada_b, norm_mix_g, norm_ffn_g, norm_out_g, ffn_w1, ffn_w2,
     sb_w_qkv, sb_w_o,
     s5_a_re, s5_a_im, s5_log_dt, s5_b_re, s5_b_im, s5_c_re, s5_c_im, s5_d, s5_w_glu,
     gdn_w_in, gdn_conv_w, gdn_a_log, gdn_dt_bias, gdn_norm_g, gdn_w_o,
     cnv_w_pw1, cnv_dw_w, cnv_dw_b, cnv_ln_g, cnv_ln_b, cnv_w_pw2) = weights
    out = {n: [] for n in ('sb_k', 'sb_v', 's5_re', 's5_im', 'gdn', 'gdn_conv', 'conv')}
    cs = jax.nn.silu(c)
    for i in range(DEPTH):
        kind, j = i % N_MIXERS, i // N_MIXERS
        mod = (cs @ ada_w[i] + ada_b[i])[:, None, :]
        sh1, sc1, gt1, sh2, sc2, gt2 = jnp.split(mod, 6, axis=-1)
        hm = rms_norm(x, norm_mix_g[i]) * (1 + sc1) + sh1
        if kind == 0:
            y, k_new, v_new = sb_mixer(hm, sb_k[j], sb_v[j], sb_w_qkv[j], sb_w_o[j])
            out['sb_k'].append(k_new)
            out['sb_v'].append(v_new)
        elif kind == 1:
            y, r_new, i_new = s5_mixer(hm, s5_re[j], s5_im[j], s5_a_re[j], s5_a_im[j],
                                       s5_log_dt[j], s5_b_re[j], s5_b_im[j], s5_c_re[j],
                                       s5_c_im[j], s5_d[j], s5_w_glu[j])
            out['s5_re'].append(r_new)
            out['s5_im'].append(i_new)
        elif kind == 2:
            y, s_new, b_new = gdn_mixer(hm, gdn_s[j], gdn_cb[j], gdn_w_in[j], gdn_conv_w[j],
                                        gdn_a_log[j], gdn_dt_bias[j], gdn_norm_g[j], gdn_w_o[j])
            out['gdn'].append(s_new)
            out['gdn_conv'].append(b_new)
        else:
            y, b_new = conv_mixer(hm, cnv_cb[j], cnv_w_pw1[j], cnv_dw_w[j], cnv_dw_b[j],
                                  cnv_ln_g[j], cnv_ln_b[j], cnv_w_pw2[j])
            out['conv'].append(b_new)
        x = x + gt1 * y
        hf = rms_norm(x, norm_ffn_g[i]) * (1 + sc2) + sh2
        x = x + gt2 * sq_relu_mlp(hf, ffn_w1[i], ffn_w2[i])
    y = rms_norm(x, norm_out_g)
    return (y, jnp.stack(out['sb_k']), jnp.stack(out['sb_v']), jnp.stack(out['s5_re']),
            jnp.stack(out['s5_im']), jnp.stack(out['gdn']), jnp.stack(out['gdn_conv']),
            jnp.stack(out['conv']))


def setup_inputs(seed: int = 0) -> dict:
    key = jax.random.key(seed)
    ks = iter(jax.random.split(key, 64))

    def nrm(shape, scale=1.0):
        return scale * jax.random.normal(next(ks), shape, F32)

    def unif(shape, lo, hi):
        return jax.random.uniform(next(ks), shape, F32, lo, hi)

    D = D_MODEL
    G, P, N = S5_GROUPS, S5_STATE, S5_GROUP
    H, dh = GDN_HEADS, GDN_HEAD_DIM
    gdn_dt = jnp.exp(unif((N_GDN, H), math.log(1e-3), math.log(1e-1)))
    return {
        'x_prompt': nrm((BATCH, SEQ, D)),
        'x_sample': nrm((DEC_BATCH, DEC_SEQ, D)),
        'c_prompt': nrm((BATCH, D)),
        'c_sample': nrm((DEC_BATCH, D)),
        'cache_sb_k': nrm((N_SB, DEC_BATCH, PAST_LEN, SB_HEADS, SB_HEAD_DIM)),
        'cache_sb_v': nrm((N_SB, DEC_BATCH, PAST_LEN, SB_HEADS, SB_HEAD_DIM)),
        'state_s5_re': nrm((N_S5, DEC_BATCH, G, P), 0.1),
        'state_s5_im': nrm((N_S5, DEC_BATCH, G, P), 0.1),
        'state_gdn': nrm((N_GDN, DEC_BATCH, H, dh, dh), 0.1),
        'state_gdn_conv': nrm((N_GDN, DEC_BATCH, GDN_CONV - 1, 3 * D)),
        'state_conv': nrm((N_CNV, DEC_BATCH, CNV_WIDTH - 1, D), 0.5),
        'ada_w': nrm((DEPTH, D, 6 * D), 0.5 * D ** -0.5),
        'ada_b': nrm((DEPTH, 6 * D), 0.02),
        'norm_mix_g': 1.0 + nrm((DEPTH, D), 0.05),
        'norm_ffn_g': 1.0 + nrm((DEPTH, D), 0.05),
        'norm_out_g': 1.0 + nrm((D,), 0.05),
        'ffn_w1': nrm((DEPTH, D, D_FF), D ** -0.5),
        'ffn_w2': nrm((DEPTH, D_FF, D), D_FF ** -0.5),
        'sb_w_qkv': nrm((N_SB, D, 3 * D), D ** -0.5),
        'sb_w_o': nrm((N_SB, D, D), D ** -0.5),
        's5_a_re': -0.5 + nrm((N_S5, G, P), 0.01),
        's5_a_im': jnp.pi * jnp.arange(P, dtype=F32) + nrm((N_S5, G, P), 0.01),
        's5_log_dt': unif((N_S5, G), math.log(1e-3), math.log(1e-1)),
        's5_b_re': nrm((N_S5, G, P, N), (2 * N) ** -0.5),
        's5_b_im': nrm((N_S5, G, P, N), (2 * N) ** -0.5),
        's5_c_re': nrm((N_S5, G, N, P), (2 * P) ** -0.5),
        's5_c_im': nrm((N_S5, G, N, P), (2 * P) ** -0.5),
        's5_d': nrm((N_S5, D)),
        's5_w_glu': nrm((N_S5, D, 2 * D), D ** -0.5),
        'gdn_w_in': nrm((N_GDN, D, 4 * D + 2 * H), D ** -0.5),
        'gdn_conv_w': nrm((N_GDN, GDN_CONV, 3 * D), GDN_CONV ** -0.5),
        'gdn_a_log': jnp.log(unif((N_GDN, H), 1.0, 16.0)),
        'gdn_dt_bias': gdn_dt + jnp.log(-jnp.expm1(-gdn_dt)),
        'gdn_norm_g': 1.0 + nrm((N_GDN, dh), 0.05),
        'gdn_w_o': nrm((N_GDN, D, D), D ** -0.5),
        'cnv_w_pw1': nrm((N_CNV, D, 2 * D), D ** -0.5),
        'cnv_dw_w': nrm((N_CNV, CNV_WIDTH, D), CNV_WIDTH ** -0.5),
        'cnv_dw_b': nrm((N_CNV, D), 0.02),
        'cnv_ln_g': 1.0 + nrm((N_CNV, D), 0.05),
        'cnv_ln_b': nrm((N_CNV, D), 0.02),
        'cnv_w_pw2': nrm((N_CNV, D, D), D ** -0.5),
    }


def reference(x_prompt, x_sample, c_prompt, c_sample, cache_sb_k, cache_sb_v,
              state_s5_re, state_s5_im, state_gdn, state_gdn_conv, state_conv,
              ada_w, ada_b, norm_mix_g, norm_ffn_g, norm_out_g, ffn_w1, ffn_w2,
              sb_w_qkv, sb_w_o,
              s5_a_re, s5_a_im, s5_log_dt, s5_b_re, s5_b_im, s5_c_re, s5_c_im, s5_d, s5_w_glu,
              gdn_w_in, gdn_conv_w, gdn_a_log, gdn_dt_bias, gdn_norm_g, gdn_w_o,
              cnv_w_pw1, cnv_dw_w, cnv_dw_b, cnv_ln_g, cnv_ln_b, cnv_w_pw2):
    weights = (ada_w, ada_b, norm_mix_g, norm_ffn_g, norm_out_g, ffn_w1, ffn_w2,
               sb_w_qkv, sb_w_o,
               s5_a_re, s5_a_im, s5_log_dt, s5_b_re, s5_b_im, s5_c_re, s5_c_im, s5_d, s5_w_glu,
               gdn_w_in, gdn_conv_w, gdn_a_log, gdn_dt_bias, gdn_norm_g, gdn_w_o,
               cnv_w_pw1, cnv_dw_w, cnv_dw_b, cnv_ln_g, cnv_ln_b, cnv_w_pw2)
    bp = x_prompt.shape[0]
    dt = x_prompt.dtype
    (y_prompt, sb_k_prompt, sb_v_prompt, s5_re_prompt, s5_im_prompt,
     gdn_prompt, gdn_conv_prompt, conv_prompt) = trunk(
        x_prompt, c_prompt,
        jnp.zeros((N_SB, bp, 0, SB_HEADS, SB_HEAD_DIM), dt),
        jnp.zeros((N_SB, bp, 0, SB_HEADS, SB_HEAD_DIM), dt),
        jnp.zeros((N_S5, bp, S5_GROUPS, S5_STATE), F32),
        jnp.zeros((N_S5, bp, S5_GROUPS, S5_STATE), F32),
        jnp.zeros((N_GDN, bp, GDN_HEADS, GDN_HEAD_DIM, GDN_HEAD_DIM), F32),
        jnp.zeros((N_GDN, bp, GDN_CONV - 1, 3 * D_MODEL), dt),
        jnp.zeros((N_CNV, bp, CNV_WIDTH - 1, D_MODEL), dt),
        weights)
    (y_sample, sb_k_sample, sb_v_sample, s5_re_sample, s5_im_sample,
     gdn_sample, gdn_conv_sample, conv_sample) = trunk(
        x_sample, c_sample, cache_sb_k, cache_sb_v, state_s5_re, state_s5_im,
        state_gdn, state_gdn_conv, state_conv, weights)
    return (y_prompt, y_sample, sb_k_prompt, sb_v_prompt, sb_k_sample, sb_v_sample,
            s5_re_prompt, s5_im_prompt, s5_re_sample, s5_im_sample,
            gdn_prompt, gdn_conv_prompt, gdn_sample, gdn_conv_sample,
            conv_prompt, conv_sample)
```

```python
import functools

import jax
import jax.numpy as jnp
from jax import lax
from jax.experimental import pallas as pl
from jax.experimental.pallas import tpu as pltpu

F32 = jnp.float32
BF16 = jnp.bfloat16

D_MODEL = 1024
DEPTH = 4
EPS = 1e-6
D_FF = 4 * D_MODEL

SB_HEADS = 16
SB_HEAD_DIM = 64
SB_BLOCK = 128

S5_GROUPS = 64
S5_GROUP = 16
S5_STATE = 64
S5_LANES = S5_GROUPS * S5_STATE
S5_GROUPS_PER_TILE = 16
S5_TILES = S5_GROUPS // S5_GROUPS_PER_TILE

GDN_HEADS = 8
GDN_HEAD_DIM = 128
GDN_CONV = 4
GDN_CHUNK = 64

CNV_WIDTH = 31
CNV_HALO = 32

LANES = 128
SUBLANES = 8
ROW_TILE = 512
FF_CHUNK = 512
V7X_VMEM_BYTES = 64 * 2 ** 20
VMEM_LIMIT = 56 * 2 ** 20


def _cparams(sem):
    return pltpu.CompilerParams(dimension_semantics=sem, vmem_limit_bytes=VMEM_LIMIT)


def _resident(shape):
    nd = len(shape)
    return pl.BlockSpec(shape, lambda *_: (0,) * nd, pipeline_mode=pl.Buffered(1))


def _dot(a, b):
    return jnp.dot(a, b, preferred_element_type=F32)


def _dot_nt(a, b):
    return lax.dot_general(a, b, (((1,), (1,)), ((), ())), preferred_element_type=F32)


def _dot_f32(a, b):
    return jnp.dot(a, b, preferred_element_type=F32, precision=lax.Precision.HIGHEST)


def _sigmoid(x):
    return 1.0 / (1.0 + jnp.exp(-x))


def _silu(x):
    return x * _sigmoid(x)


def _softplus(x):
    return jnp.maximum(x, 0.0) + jnp.log(1.0 + jnp.exp(-jnp.abs(x)))


def _rms(x):
    return x * lax.rsqrt(jnp.mean(x * x, axis=-1, keepdims=True) + EPS)


def _norm_mod(x, g, sc, sh):
    return _rms(x) * g * (1.0 + sc) + sh


def _ada_kernel(c_ref, w_ref, b_ref, o_ref):
    cs = _silu(c_ref[...]).astype(BF16)
    o_ref[0] = _dot(cs, w_ref[0].astype(BF16)) + b_ref[0]


def _ada(c_all, ada_w, ada_b):
    nb = c_all.shape[0]
    ncol = ada_w.shape[-1] // D_MODEL
    return pl.pallas_call(
        _ada_kernel,
        grid=(DEPTH, ncol),
        in_specs=[
            pl.BlockSpec((nb, D_MODEL), lambda l, j: (0, 0)),
            pl.BlockSpec((1, D_MODEL, D_MODEL), lambda l, j: (l, 0, j)),
            pl.BlockSpec((1, 1, D_MODEL), lambda l, j: (l, 0, j)),
        ],
        out_specs=pl.BlockSpec((1, nb, D_MODEL), lambda l, j: (l, 0, j)),
        out_shape=jax.ShapeDtypeStruct((DEPTH, nb, ncol * D_MODEL), F32),
        compiler_params=_cparams(("arbitrary", "arbitrary")),
        name="ada_mod",
    )(c_all, ada_w, ada_b.reshape(DEPTH, 1, -1))


def _row_specs(tm):
    x_spec = pl.BlockSpec((1, tm, D_MODEL), lambda b, t: (b, t, 0))
    mod_spec = pl.BlockSpec((1, 6, D_MODEL), lambda b, t: (b, 0, 0))
    return x_spec, mod_spec


def _mixer_in(x_ref, mod_ref, g_ref):
    return _norm_mod(x_ref[0], g_ref[...], mod_ref[0, 1:2, :], mod_ref[0, 0:1, :]).astype(BF16)


def _qkv_kernel(x_ref, mod_ref, g_ref, w_ref, q_ref, k_ref, v_ref, kb_ref, vb_ref):
    hm = _mixer_in(x_ref, mod_ref, g_ref)
    q_ref[0] = (_dot(hm, w_ref[:, 0:D_MODEL]) * (SB_HEAD_DIM ** -0.5)).astype(BF16)
    k = _dot(hm, w_ref[:, D_MODEL:2 * D_MODEL])
    k_ref[0] = k
    kb_ref[0] = k.astype(BF16)
    v = _dot(hm, w_ref[:, 2 * D_MODEL:3 * D_MODEL])
    v_ref[0] = v
    vb_ref[0] = v.astype(BF16)


def _qkv_proj(x, mod, g, w):
    B, T, _ = x.shape
    tm = min(ROW_TILE, T)
    x_spec, mod_spec = _row_specs(tm)
    out = lambda dt: jax.ShapeDtypeStruct((B, T, D_MODEL), dt)
    return pl.pallas_call(
        _qkv_kernel,
        grid=(B, T // tm),
        in_specs=[x_spec, mod_spec, _resident((1, D_MODEL)), _resident((D_MODEL, 3 * D_MODEL))],
        out_specs=[x_spec] * 5,
        out_shape=[out(BF16), out(F32), out(F32), out(BF16), out(BF16)],
        compiler_params=_cparams(("parallel", "parallel")),
        name="sb_qkv",
    )(x, mod, g, w)


def _gdn_in_kernel(x_ref, mod_ref, g_ref, w_ref, wab_ref, qkv_ref, z_ref, ab_ref):
    hm = _mixer_in(x_ref, mod_ref, g_ref)
    for j in range(3):
        cols = slice(j * D_MODEL, (j + 1) * D_MODEL)
        qkv_ref[0, :, cols] = _dot(hm, w_ref[:, cols])
    z_ref[0] = _dot(hm, w_ref[:, 3 * D_MODEL:4 * D_MODEL])
    ab_ref[0] = _dot(hm, wab_ref[...])


def _gdn_in_proj(x, mod, g, w, wab):
    B, T, _ = x.shape
    tm = min(ROW_TILE, T)
    x_spec, mod_spec = _row_specs(tm)
    return pl.pallas_call(
        _gdn_in_kernel,
        grid=(B, T // tm),
        in_specs=[x_spec, mod_spec, _resident((1, D_MODEL)), _resident((D_MODEL, 4 * D_MODEL)),
                  _resident((D_MODEL, LANES))],
        out_specs=[pl.BlockSpec((1, tm, 3 * D_MODEL), lambda b, t: (b, t, 0)), x_spec,
                   pl.BlockSpec((1, tm, LANES), lambda b, t: (b, t, 0))],
        out_shape=[jax.ShapeDtypeStruct((B, T, 3 * D_MODEL), F32), jax.ShapeDtypeStruct((B, T, D_MODEL), F32),
                   jax.ShapeDtypeStruct((B, T, LANES), F32)],
        compiler_params=_cparams(("parallel", "parallel")),
        name="gdn_in",
    )(x, mod, g, w, wab)


def _pw1_kernel(x_ref, mod_ref, g_ref, w_ref, u_ref):
    hm = _mixer_in(x_ref, mod_ref, g_ref)
    u_ref[0] = _dot(hm, w_ref[:, 0:D_MODEL]) * _sigmoid(_dot(hm, w_ref[:, D_MODEL:2 * D_MODEL]))


def _pw1_proj(x, mod, g, w):
    B, T, _ = x.shape
    tm = min(ROW_TILE, T)
    x_spec, mod_spec = _row_specs(tm)
    return pl.pallas_call(
        _pw1_kernel,
        grid=(B, T // tm),
        in_specs=[x_spec, mod_spec, _resident((1, D_MODEL)), _resident((D_MODEL, 2 * D_MODEL))],
        out_specs=x_spec,
        out_shape=jax.ShapeDtypeStruct((B, T, D_MODEL), F32),
        compiler_params=_cparams(("parallel", "parallel")),
        name="cnv_pw1",
    )(x, mod, g, w)


def _tail_kernel(x_ref, y_ref, mod_ref, wpre_ref, g_ref, w1_ref, w2_ref, gout_ref, o_ref, acc_ref, *, glu, final):
    pre = _dot(y_ref[0], wpre_ref[...])
    if glu:
        pre = pre[:, 0:D_MODEL] * _sigmoid(pre[:, D_MODEL:2 * D_MODEL])
    x1 = x_ref[0] + mod_ref[0, 2:3, :] * pre
    o_ref[0] = x1
    hf = _norm_mod(x1, g_ref[...], mod_ref[0, 4:5, :], mod_ref[0, 3:4, :]).astype(BF16)
    acc_ref[...] = jnp.zeros_like(acc_ref)

    def ff_step(k, carry):
        a = jnp.maximum(_dot(hf, w1_ref[k]), 0.0)
        acc_ref[...] += _dot((a * a).astype(BF16), w2_ref[k])
        return carry

    lax.fori_loop(0, D_FF // FF_CHUNK, ff_step, 0)
    x2 = o_ref[0] + mod_ref[0, 5:6, :] * acc_ref[...]
    if final:
        x2 = _rms(x2) * gout_ref[...]
    o_ref[0] = x2


def _tail(x, y, mod, w_pre, g_ffn, w1c, w2c, g_out, *, glu, final):
    B, T, _ = x.shape
    tm = min(ROW_TILE, T)
    x_spec, mod_spec = _row_specs(tm)
    ky = y.shape[-1]
    nff = D_FF // FF_CHUNK
    return pl.pallas_call(
        functools.partial(_tail_kernel, glu=glu, final=final),
        grid=(B, T // tm),
        in_specs=[x_spec, pl.BlockSpec((1, tm, ky), lambda b, t: (b, t, 0)), mod_spec,
                  _resident(w_pre.shape), _resident((1, D_MODEL)),
                  _resident((nff, D_MODEL, FF_CHUNK)), _resident((nff, FF_CHUNK, D_MODEL)),
                  _resident((1, D_MODEL))],
        out_specs=x_spec,
        out_shape=jax.ShapeDtypeStruct((B, T, D_MODEL), F32),
        scratch_shapes=[pltpu.VMEM((tm, D_MODEL), F32)],
        compiler_params=_cparams(("parallel", "parallel")),
        name="tail",
    )(x, y, mod, w_pre, g_ffn, w1c, w2c, g_out)


def _make_attn_kernel(tq, past, bq, bk):
    nq = tq // bq
    npast = past // bk
    hd = SB_HEAD_DIM

    def tri(n):
        r = lax.broadcasted_iota(jnp.int32, (n, n), 0)
        c = lax.broadcasted_iota(jnp.int32, (n, n), 1)
        return r, c

    def kernel(*refs):
        if past:
            q_ref, k_ref, v_ref, kp_ref, vp_ref, o_ref = refs
        else:
            q_ref, k_ref, v_ref, o_ref = refs
        r, c = tri(bq)
        strict = c < r
        m_new = (r >= c).astype(BF16)
        if past:
            rp, cp = tri(bk)
            m_past = (rp >= cp).astype(BF16)

        def block(qh, kh, vh, carry, acc, m, mask):
            z = _dot_nt(qh, kh)
            sp = _softplus(z)
            if mask is not None:
                sp = jnp.where(mask, sp, 0.0)
            hi = sp.astype(BF16)
            lo = (sp - hi.astype(F32)).astype(BF16)
            suffix = _dot(hi, m) + _dot(lo, m)
            w = jnp.exp(z - suffix - carry)
            if mask is not None:
                w = jnp.where(mask, w, 0.0)
            acc = acc + _dot(w.astype(BF16), vh)
            carry = carry + jnp.sum(sp, axis=-1, keepdims=True)
            return carry, acc

        def q_block(i, _):
            r0 = pl.multiple_of(i * bq, bq)
            for h in range(LANES // hd):
                ls = slice(h * hd, (h + 1) * hd)
                qh = q_ref[0, pl.ds(r0, bq), ls]
                carry = jnp.zeros((bq, 1), F32)
                acc = jnp.zeros((bq, hd), F32)
                carry, acc = block(qh, k_ref[0, pl.ds(r0, bq), ls], v_ref[0, pl.ds(r0, bq), ls],
                                   carry, acc, m_new, strict)

                def new_block(jj, ca):
                    k0 = pl.multiple_of((i - 1 - jj) * bq, bq)
                    return block(qh, k_ref[0, pl.ds(k0, bq), ls], v_ref[0, pl.ds(k0, bq), ls],
                                 ca[0], ca[1], m_new, None)

                carry, acc = lax.fori_loop(0, i, new_block, (carry, acc))
                if past:
                    def past_block(jj, ca):
                        k0 = pl.multiple_of((npast - 1 - jj) * bk, bk)
                        return block(qh, kp_ref[0, pl.ds(k0, bk), ls], vp_ref[0, pl.ds(k0, bk), ls],
                                     ca[0], ca[1], m_past, None)

                    carry, acc = lax.fori_loop(0, npast, past_block, (carry, acc))
                o_ref[0, pl.ds(r0, bq), ls] = acc.astype(BF16)
            return 0

        lax.fori_loop(0, nq, q_block, 0)

    return kernel


def _sb_attention(q, k, v, k_past=None, v_past=None):
    B, T, _ = q.shape
    past = 0 if k_past is None else k_past.shape[1]
    bq = min(SB_BLOCK, T)
    seq_spec = pl.BlockSpec((1, T, LANES), lambda b, h: (b, 0, h))
    in_specs = [seq_spec] * 3
    args = [q, k, v]
    if past:
        in_specs += [pl.BlockSpec((1, past, LANES), lambda b, h: (b, 0, h))] * 2
        args += [k_past, v_past]
    return pl.pallas_call(
        _make_attn_kernel(T, past, bq, SB_BLOCK),
        grid=(B, D_MODEL // LANES),
        in_specs=in_specs,
        out_specs=seq_spec,
        out_shape=jax.ShapeDtypeStruct((B, T, D_MODEL), BF16),
        compiler_params=_cparams(("parallel", "parallel")),
        name="sb_attn",
    )(*args)


def _gelu_tanh(x):
    return 0.5 * x * (1.0 + jnp.tanh(0.7978845608028654 * (x + 0.044715 * (x * x * x))))


def _make_s5_kernel(c, bb, lane_chunk):
    rows = c * bb
    tile_lanes = S5_GROUPS_PER_TILE * S5_STATE
    tile_cols = S5_GROUPS_PER_TILE * S5_GROUP

    def kernel(x_ref, sh_ref, sc_ref, g_ref, wre_ref, wim_ref, are_ref, aim_ref, cw_ref, d_ref,
               s0r_ref, s0i_ref, y_ref, sr_ref, si_ref, hm_ref, xr_ref, xi_ref):
        @pl.when(pl.program_id(1) == 0)
        def _():
            sr_ref[...] = s0r_ref[...]
            si_ref[...] = s0i_ref[...]

        hm = _norm_mod(x_ref[...], g_ref[...], sc_ref[...][None], sh_ref[...][None])
        hm_ref[...] = hm.reshape(rows, D_MODEL)
        for kt in range(S5_TILES):
            hb = hm_ref[:, kt * tile_cols:(kt + 1) * tile_cols].astype(BF16)
            xr_ref[:, kt * tile_lanes:(kt + 1) * tile_lanes] = _dot(hb, wre_ref[kt])
            xi_ref[:, kt * tile_lanes:(kt + 1) * tile_lanes] = _dot(hb, wim_ref[kt])

        for lc in range(S5_LANES // lane_chunk):
            ls = slice(lc * lane_chunk, (lc + 1) * lane_chunk)
            a_r = are_ref[:, ls]
            a_i = aim_ref[:, ls]

            def step(t, carry):
                s_r, s_i = carry
                r0 = pl.multiple_of(t * bb, bb)
                n_r = a_r * s_r - a_i * s_i + xr_ref[pl.ds(r0, bb), ls]
                n_i = a_r * s_i + a_i * s_r + xi_ref[pl.ds(r0, bb), ls]
                xr_ref[pl.ds(r0, bb), ls] = n_r
                xi_ref[pl.ds(r0, bb), ls] = n_i
                return n_r, n_i

            s_r, s_i = lax.fori_loop(0, c, step, (sr_ref[:, ls], si_ref[:, ls]))
            sr_ref[:, ls] = s_r
            si_ref[:, ls] = s_i

        for nt in range(S5_TILES):
            ls = slice(nt * tile_lanes, (nt + 1) * tile_lanes)
            cs = slice(nt * tile_cols, (nt + 1) * tile_cols)
            st = jnp.concatenate([xr_ref[:, ls].astype(BF16), xi_ref[:, ls].astype(BF16)], axis=1)
            yv = _dot(st, cw_ref[nt]) + d_ref[:, cs] * hm_ref[:, cs]
            y_ref[:, :, cs] = _gelu_tanh(yv).astype(BF16).reshape(c, bb, tile_cols)

    return kernel


def _s5_scan(xt, sh, sc, g, wre, wim, a_re, a_im, cw, d, s0r, s0i):
    T, B, _ = xt.shape
    bb = min(B, 16)
    c = min(32, T)
    x_spec = pl.BlockSpec((c, bb, D_MODEL), lambda b, t: (t, b, 0))
    b_spec = pl.BlockSpec((bb, D_MODEL), lambda b, t: (b, 0))
    s_spec = pl.BlockSpec((bb, S5_LANES), lambda b, t: (b, 0))
    tile_lanes = S5_GROUPS_PER_TILE * S5_STATE
    tile_cols = S5_GROUPS_PER_TILE * S5_GROUP
    return pl.pallas_call(
        _make_s5_kernel(c, bb, 512),
        grid=(B // bb, T // c),
        in_specs=[x_spec, b_spec, b_spec, _resident((1, D_MODEL)),
                  _resident((S5_TILES, tile_cols, tile_lanes)), _resident((S5_TILES, tile_cols, tile_lanes)),
                  _resident((1, S5_LANES)), _resident((1, S5_LANES)),
                  _resident((S5_TILES, 2 * tile_lanes, tile_cols)), _resident((1, D_MODEL)),
                  s_spec, s_spec],
        out_specs=[x_spec, s_spec, s_spec],
        out_shape=[jax.ShapeDtypeStruct((T, B, D_MODEL), BF16), jax.ShapeDtypeStruct((B, S5_LANES), F32),
                   jax.ShapeDtypeStruct((B, S5_LANES), F32)],
        scratch_shapes=[pltpu.VMEM((c * bb, D_MODEL), F32), pltpu.VMEM((c * bb, S5_LANES), F32),
                        pltpu.VMEM((c * bb, S5_LANES), F32)],
        compiler_params=_cparams(("parallel", "arbitrary")),
        name="s5_scan",
    )(xt, sh, sc, g, wre, wim, a_re, a_im, cw, d, s0r, s0i)


def _s5_params(a_re, a_im, log_dt, b_re, b_im, c_re, c_im):
    dt = jnp.exp(log_dt.astype(F32))[:, None]
    ar, ai = a_re.astype(F32), a_im.astype(F32)
    mag = jnp.exp(ar * dt)
    abar_re, abar_im = mag * jnp.cos(ai * dt), mag * jnp.sin(ai * dt)
    den = ar * ar + ai * ai
    f_re = ((abar_re - 1.0) * ar + abar_im * ai) / den
    f_im = (abar_im * ar - (abar_re - 1.0) * ai) / den
    br, bi = b_re.astype(F32), b_im.astype(F32)
    bb_re = f_re[..., None] * br - f_im[..., None] * bi
    bb_im = f_re[..., None] * bi + f_im[..., None] * br
    gt = S5_GROUPS_PER_TILE
    eye = jnp.eye(gt, dtype=F32)

    def pack_in(w):
        w = w.reshape(S5_TILES, gt, S5_STATE, S5_GROUP)
        return jnp.einsum('kgpn,gh->kgnhp', w, eye).reshape(S5_TILES, gt * S5_GROUP, gt * S5_STATE).astype(BF16)

    def pack_out(w):
        w = w.reshape(S5_TILES, gt, S5_GROUP, S5_STATE)
        return jnp.einsum('kgnp,gh->kgphn', w, eye).reshape(S5_TILES, gt * S5_STATE, gt * S5_GROUP)

    cw = jnp.concatenate([pack_out(c_re.astype(F32)), -pack_out(c_im.astype(F32))], axis=1).astype(BF16)
    return (pack_in(bb_re), pack_in(bb_im), abar_re.reshape(1, S5_LANES), abar_im.reshape(1, S5_LANES), cw)


def _make_gdn_kernel(T, c):
    n = T // c
    dh = GDN_HEAD_DIM
    ng = c // SUBLANES
    pad0 = SUBLANES
    taps = GDN_CONV

    def kernel(q_ref, k_ref, v_ref, z_ref, ab_ref, cq_ref, ck_ref, cv_ref, wq_ref, wk_ref, wv_ref,
               alog_ref, dtb_ref, ng_ref, s0_ref, o_ref, s_ref, pq_ref, pk_ref, pv_ref):
        h = pl.program_id(1)
        s_ref[0, 0] = s0_ref[0, 0]
        for pref, cref in ((pq_ref, cq_ref), (pk_ref, ck_ref), (pv_ref, cv_ref)):
            pref[0:pad0, :] = jnp.zeros((pad0, dh), F32)
            pref[pad0 - (taps - 1):pad0, :] = cref[0]

        r = lax.broadcasted_iota(jnp.int32, (c, c), 0)
        col = lax.broadcasted_iota(jnp.int32, (c, c), 1)
        incl = r >= col
        strict = r > col
        eye = r == col
        l_incl = incl.astype(F32)
        lane = lax.broadcasted_iota(jnp.int32, (1, LANES), 1)
        sel_a = lane == h
        sel_b = lane == h + GDN_HEADS
        neg_decay_rate = -jnp.exp(alog_ref[...])

        def conv(src_ref, pref, w_ref, t0):
            pref[pad0:pad0 + c, :] = src_ref[0, pl.ds(t0, c), :]
            acc = w_ref[0:1, :] * pref[pl.ds(pad0 - (taps - 1), c), :]
            for i in range(1, taps):
                acc = acc + w_ref[i:i + 1, :] * pref[pl.ds(pad0 - (taps - 1) + i, c), :]
            pref[0:pad0, :] = pref[c:c + pad0, :]
            return _silu(acc)

        def pick(sel, m):
            return jnp.sum(jnp.where(sel, m, 0.0), axis=-1, keepdims=True)

        def chunk(ci, carry):
            t0 = pl.multiple_of(ci * c, c)
            q = conv(q_ref, pq_ref, wq_ref, t0)
            k = conv(k_ref, pk_ref, wk_ref, t0)
            v = conv(v_ref, pv_ref, wv_ref, t0)
            q = q * lax.rsqrt(jnp.sum(q * q, axis=-1, keepdims=True) + EPS) * (dh ** -0.5)
            k = k * lax.rsqrt(jnp.sum(k * k, axis=-1, keepdims=True) + EPS)

            ab = ab_ref[0, pl.ds(t0, c), :]
            g_all = neg_decay_rate * _softplus(ab + dtb_ref[...])
            gcum = pick(sel_a, _dot_f32(l_incl, g_all))
            beta = pick(sel_b, _sigmoid(ab))
            gcum_row = jnp.sum(jnp.where(eye, gcum, 0.0), axis=0, keepdims=True)
            decay = jnp.where(incl, jnp.exp(jnp.where(incl, gcum - gcum_row, 0.0)), 0.0)

            kb = k.astype(BF16)
            kk = _dot_nt(kb, kb)
            qk = _dot_nt(q.astype(BF16), kb) * decay
            a = jnp.where(strict, beta * kk * decay, 0.0)

            inv = [jnp.where(eye[g * SUBLANES:(g + 1) * SUBLANES], 1.0, 0.0).astype(F32) for g in range(ng)]
            a_g = [a[g * SUBLANES:(g + 1) * SUBLANES] for g in range(ng)]
            for j in range(c - 1):
                g0, jr = divmod(j, SUBLANES)
                row = inv[g0][jr:jr + 1, :]
                for g in range(g0, ng):
                    inv[g] = inv[g] - a_g[g][:, j:j + 1] * row
            inv = jnp.concatenate(inv, axis=0)

            eg = jnp.exp(gcum)
            rhs = jnp.concatenate([beta * v, (beta * eg) * k], axis=1)
            sol = _dot_f32(inv, rhs)
            u, w = sol[:, 0:dh], sol[:, dh:2 * dh]

            s = s_ref[0, 0]
            sb = s.astype(BF16)
            v_new = u - _dot(w.astype(BF16), sb)
            vb = v_new.astype(BF16)
            o = _dot((q * eg).astype(BF16), sb) + _dot(qk.astype(BF16), vb)
            g_last = gcum[c - 1:c, :]
            kd = (k * jnp.exp(g_last - gcum)).astype(BF16)
            s_ref[0, 0] = s * jnp.exp(g_last) + lax.dot_general(
                kd, vb, (((0,), (0,)), ((), ())), preferred_element_type=F32)

            o = _rms(o) * ng_ref[...] * _silu(z_ref[0, pl.ds(t0, c), :])
            o_ref[0, pl.ds(t0, c), :] = o.astype(BF16)
            return carry

        lax.fori_loop(0, n, chunk, 0)

    return kernel


def _gdn_core(qkv, z, ab, cbuf, conv_w, alog_row, dtb_row, norm_g, s0):
    B, T, _ = qkv.shape
    c = min(GDN_CHUNK, T)
    H, dh = GDN_HEADS, GDN_HEAD_DIM

    def seq(off):
        return pl.BlockSpec((1, T, dh), lambda b, h: (b, 0, off + h))

    def hist(off):
        return pl.BlockSpec((1, GDN_CONV - 1, dh), lambda b, h: (b, 0, off + h))

    def wconv(off):
        return pl.BlockSpec((GDN_CONV, dh), lambda b, h: (0, off + h))

    st_spec = pl.BlockSpec((1, 1, dh, dh), lambda b, h: (b, h, 0, 0))
    pad = pltpu.VMEM((c + SUBLANES, dh), F32)
    return pl.pallas_call(
        _make_gdn_kernel(T, c),
        grid=(B, H),
        in_specs=[seq(0), seq(H), seq(2 * H), seq(0),
                  pl.BlockSpec((1, T, LANES), lambda b, h: (b, 0, 0)),
                  hist(0), hist(H), hist(2 * H), wconv(0), wconv(H), wconv(2 * H),
                  _resident((1, LANES)), _resident((1, LANES)), _resident((1, dh)), st_spec],
        out_specs=[seq(0), st_spec],
        out_shape=[jax.ShapeDtypeStruct((B, T, D_MODEL), BF16), jax.ShapeDtypeStruct((B, H, dh, dh), F32)],
        scratch_shapes=[pad, pad, pad],
        compiler_params=_cparams(("parallel", "parallel")),
        name="gdn_core",
    )(qkv, qkv, qkv, z, ab, cbuf, cbuf, cbuf, conv_w, conv_w, conv_w, alog_row, dtb_row, norm_g, s0)


def _make_conv_kernel(T, tq, rsub, lchunk):
    halo = CNV_HALO
    first_hist = halo - (CNV_WIDTH - 1)

    def kernel(u_ref, cb_ref, w_ref, b_ref, lg_ref, lb_ref, y_ref, uc_ref, acc_ref):
        t = pl.program_id(1)
        t0 = pl.multiple_of(t * tq, tq)

        @pl.when(t == 0)
        def _():
            uc_ref[0:halo, :] = jnp.zeros((halo, D_MODEL), F32)
            uc_ref[first_hist:halo, :] = cb_ref[0]

        @pl.when(t > 0)
        def _():
            uc_ref[0:halo, :] = u_ref[0, pl.ds(t0 - halo, halo), :]

        uc_ref[halo:halo + tq, :] = u_ref[0, pl.ds(t0, tq), :]

        for lc in range(D_MODEL // lchunk):
            ls = slice(lc * lchunk, (lc + 1) * lchunk)

            for r0 in range(0, tq, rsub):
                acc = w_ref[0:1, ls] * uc_ref[r0 + first_hist:r0 + first_hist + rsub, ls]
                for i in range(1, CNV_WIDTH):
                    acc = acc + w_ref[i:i + 1, ls] * uc_ref[r0 + first_hist + i:r0 + first_hist + i + rsub, ls]
                acc_ref[r0:r0 + rsub, ls] = acc + b_ref[:, ls]

        y = acc_ref[...]
        yc = y - jnp.mean(y, axis=-1, keepdims=True)
        var = jnp.mean(yc * yc, axis=-1, keepdims=True)
        y_ref[0] = _silu(yc * lax.rsqrt(var + EPS) * lg_ref[...] + lb_ref[...]).astype(BF16)

    return kernel


def _conv_core(u, cbuf, dw_w, dw_b, ln_g, ln_b):
    B, T, _ = u.shape
    tq = min(256, T)
    rsub = min(64, tq)
    row = _resident((1, D_MODEL))
    return pl.pallas_call(
        _make_conv_kernel(T, tq, rsub, 256),
        grid=(B, T // tq),
        in_specs=[pl.BlockSpec((1, T, D_MODEL), lambda b, t: (b, 0, 0)),
                  pl.BlockSpec((1, CNV_WIDTH - 1, D_MODEL), lambda b, t: (b, 0, 0)),
                  _resident((CNV_WIDTH, D_MODEL)), row, row, row],
        out_specs=pl.BlockSpec((1, tq, D_MODEL), lambda b, t: (b, t, 0)),
        out_shape=jax.ShapeDtypeStruct((B, T, D_MODEL), BF16),
        scratch_shapes=[pltpu.VMEM((tq + CNV_HALO, D_MODEL), F32), pltpu.VMEM((tq, D_MODEL), F32)],
        compiler_params=_cparams(("parallel", "arbitrary")),
        name="cnv_core",
    )(u, cbuf, dw_w, dw_b, ln_g, ln_b)


def _row(v):
    return v.astype(F32).reshape(1, -1)


def _pad_lanes(v):
    return jnp.pad(v.astype(F32), (0, LANES - v.shape[0])).reshape(1, LANES)


def _prep_weights(p):
    nff = D_FF // FF_CHUNK
    w = {}
    w['w1'] = [p['ffn_w1'][i].astype(BF16).reshape(D_MODEL, nff, FF_CHUNK).transpose(1, 0, 2) for i in range(DEPTH)]
    w['w2'] = [p['ffn_w2'][i].astype(BF16).reshape(nff, FF_CHUNK, D_MODEL) for i in range(DEPTH)]
    w['sb_qkv'] = p['sb_w_qkv'][0].astype(BF16)
    w['sb_o'] = p['sb_w_o'][0].astype(BF16)
    w['s5'] = _s5_params(p['s5_a_re'][0], p['s5_a_im'][0], p['s5_log_dt'][0], p['s5_b_re'][0], p['s5_b_im'][0],
                         p['s5_c_re'][0], p['s5_c_im'][0])
    w['s5_glu'] = p['s5_w_glu'][0].astype(BF16)
    w_in = p['gdn_w_in'][0]
    w['gdn_in'] = w_in[:, :4 * D_MODEL].astype(BF16)
    w['gdn_ab'] = jnp.pad(w_in[:, 4 * D_MODEL:], ((0, 0), (0, LANES - 2 * GDN_HEADS))).astype(BF16)
    w['gdn_o'] = p['gdn_w_o'][0].astype(BF16)
    w['pw1'] = p['cnv_w_pw1'][0].astype(BF16)
    w['pw2'] = p['cnv_w_pw2'][0].astype(BF16)
    return w


def _trunk(x, mod, sb_k, sb_v, s5_re, s5_im, gdn_s, gdn_cb, cnv_cb, p, w):
    B, T, _ = x.shape
    out = {}
    g_out = _row(p['norm_out_g'])

    def tail(i, x, y, w_pre, glu=False):
        return _tail(x, y, mod[i], w_pre, _row(p['norm_ffn_g'][i]), w['w1'][i], w['w2'][i], g_out,
                     glu=glu, final=(i == DEPTH - 1))

    q, k, v, kb, vb = _qkv_proj(x, mod[0], _row(p['norm_mix_g'][0]), w['sb_qkv'])
    if sb_k is None:
        o = _sb_attention(q, kb, vb)
    else:
        P = sb_k.shape[1]
        o = _sb_attention(q, kb, vb, sb_k.reshape(B, P, D_MODEL).astype(BF16),
                          sb_v.reshape(B, P, D_MODEL).astype(BF16))
    out['sb_k'] = k.reshape(1, B, T, SB_HEADS, SB_HEAD_DIM)
    out['sb_v'] = v.reshape(1, B, T, SB_HEADS, SB_HEAD_DIM)
    x = tail(0, x, o, w['sb_o'])

    wre, wim, a_re, a_im, cw = w['s5']
    yt, s_re, s_im = _s5_scan(x.transpose(1, 0, 2), mod[1][:, 0, :], mod[1][:, 1, :], _row(p['norm_mix_g'][1]),
                              wre, wim, a_re, a_im, cw, _row(p['s5_d'][0]),
                              s5_re.reshape(B, S5_LANES), s5_im.reshape(B, S5_LANES))
    out['s5_re'] = s_re.reshape(1, B, S5_GROUPS, S5_STATE)
    out['s5_im'] = s_im.reshape(1, B, S5_GROUPS, S5_STATE)
    x = tail(1, x, yt.transpose(1, 0, 2), w['s5_glu'], glu=True)

    qkv, z, ab = _gdn_in_proj(x, mod[2], _row(p['norm_mix_g'][2]), w['gdn_in'], w['gdn_ab'])
    o, s_new = _gdn_core(qkv, z, ab, gdn_cb, p['gdn_conv_w'][0].astype(F32), _pad_lanes(p['gdn_a_log'][0]),
                         _pad_lanes(p['gdn_dt_bias'][0]), _row(p['gdn_norm_g'][0]), gdn_s)
    out['gdn'] = s_new[None]
    hist = jnp.concatenate([gdn_cb, qkv], axis=1) if T < GDN_CONV - 1 else qkv
    out['gdn_conv'] = hist[None, :, hist.shape[1] - (GDN_CONV - 1):]
    x = tail(2, x, o, w['gdn_o'])

    u = _pw1_proj(x, mod[3], _row(p['norm_mix_g'][3]), w['pw1'])
    ya = _conv_core(u, cnv_cb, p['cnv_dw_w'][0].astype(F32), _row(p['cnv_dw_b'][0]), _row(p['cnv_ln_g'][0]),
                    _row(p['cnv_ln_b'][0]))
    hist = jnp.concatenate([cnv_cb, u], axis=1) if T < CNV_WIDTH - 1 else u
    out['conv'] = hist[None, :, hist.shape[1] - (CNV_WIDTH - 1):]
    x = tail(3, x, ya, w['pw2'])
    return x, out


def kernel(x_prompt, x_sample, c_prompt, c_sample, cache_sb_k, cache_sb_v, state_s5_re, state_s5_im, state_gdn, state_gdn_conv, state_conv, ada_w, ada_b, norm_mix_g, norm_ffn_g, norm_out_g, ffn_w1, ffn_w2, sb_w_qkv, sb_w_o, s5_a_re, s5_a_im, s5_log_dt, s5_b_re, s5_b_im, s5_c_re, s5_c_im, s5_d, s5_w_glu, gdn_w_in, gdn_conv_w, gdn_a_log, gdn_dt_bias, gdn_norm_g, gdn_w_o, cnv_w_pw1, cnv_dw_w, cnv_dw_b, cnv_ln_g, cnv_ln_b, cnv_w_pw2):
    p = dict(norm_mix_g=norm_mix_g, norm_ffn_g=norm_ffn_g, norm_out_g=norm_out_g, ffn_w1=ffn_w1, ffn_w2=ffn_w2,
             sb_w_qkv=sb_w_qkv, sb_w_o=sb_w_o, s5_a_re=s5_a_re, s5_a_im=s5_a_im, s5_log_dt=s5_log_dt,
             s5_b_re=s5_b_re, s5_b_im=s5_b_im, s5_c_re=s5_c_re, s5_c_im=s5_c_im, s5_d=s5_d, s5_w_glu=s5_w_glu,
             gdn_w_in=gdn_w_in, gdn_conv_w=gdn_conv_w, gdn_a_log=gdn_a_log, gdn_dt_bias=gdn_dt_bias,
             gdn_norm_g=gdn_norm_g, gdn_w_o=gdn_w_o, cnv_w_pw1=cnv_w_pw1, cnv_dw_w=cnv_dw_w, cnv_dw_b=cnv_dw_b,
             cnv_ln_g=cnv_ln_g, cnv_ln_b=cnv_ln_b, cnv_w_pw2=cnv_w_pw2)
    w = _prep_weights(p)
    bp, bs = x_prompt.shape[0], x_sample.shape[0]
    mod = _ada(jnp.concatenate([c_prompt, c_sample], axis=0), ada_w, ada_b)
    mod = mod.reshape(DEPTH, bp + bs, 6, D_MODEL)
    mod_p = [mod[i, :bp] for i in range(DEPTH)]
    mod_s = [mod[i, bp:] for i in range(DEPTH)]

    yp, op = _trunk(
        x_prompt, mod_p, None, None,
        jnp.zeros((bp, S5_GROUPS, S5_STATE), F32), jnp.zeros((bp, S5_GROUPS, S5_STATE), F32),
        jnp.zeros((bp, GDN_HEADS, GDN_HEAD_DIM, GDN_HEAD_DIM), F32),
        jnp.zeros((bp, GDN_CONV - 1, 3 * D_MODEL), F32), jnp.zeros((bp, CNV_WIDTH - 1, D_MODEL), F32), p, w)
    ys, os_ = _trunk(
        x_sample, mod_s, cache_sb_k[0], cache_sb_v[0], state_s5_re[0], state_s5_im[0],
        state_gdn[0], state_gdn_conv[0], state_conv[0], p, w)
    return (yp, ys, op['sb_k'], op['sb_v'], os_['sb_k'], os_['sb_v'],
            op['s5_re'], op['s5_im'], os_['s5_re'], os_['s5_im'],
            op['gdn'], op['gdn_conv'], os_['gdn'], os_['gdn_conv'],
            op['conv'], os_['conv'])
```

```python
import functools

import jax
import jax.numpy as jnp
from jax import lax
from jax.experimental import pallas as pl
from jax.experimental.pallas import tpu as pltpu

F32 = jnp.float32
BF16 = jnp.bfloat16

D_MODEL = 1024
DEPTH = 4
EPS = 1e-6
D_FF = 4 * D_MODEL

SB_HEADS = 16
SB_HEAD_DIM = 64
SB_Q_ROWS = 512
SB_KEY_BLOCK = 256
SB_SKIP = 104.0

S5_GROUPS = 64
S5_GROUP = 16
S5_STATE = 64
S5_LANES = S5_GROUPS * S5_STATE
S5_GROUPS_PER_TILE = 16
S5_TILES = S5_GROUPS // S5_GROUPS_PER_TILE

GDN_HEADS = 8
GDN_HEAD_DIM = 128
GDN_CONV = 4
GDN_CHUNK = 64
GDN_HEADS_PER_STEP = 4
GDN_CHUNKS_PER_STEP = 4
GDN_SOLVE_BLOCK = 16

CNV_WIDTH = 31
CNV_HALO = 32

LANES = 128
SUBLANES = 8
ROW_TILE = 512
FF_CHUNK = 512
V7X_VMEM_BYTES = 64 * 2 ** 20
VMEM_LIMIT = 56 * 2 ** 20


def _cparams(sem):
    return pltpu.CompilerParams(dimension_semantics=sem, vmem_limit_bytes=VMEM_LIMIT)


def _resident(shape):
    nd = len(shape)
    return pl.BlockSpec(shape, lambda *_: (0,) * nd, pipeline_mode=pl.Buffered(1))


def _dot(a, b):
    return jnp.dot(a, b, preferred_element_type=F32)


def _dot_nt(a, b):
    return lax.dot_general(a, b, (((1,), (1,)), ((), ())), preferred_element_type=F32)


def _dot_f32(a, b):
    return jnp.dot(a, b, preferred_element_type=F32, precision=lax.Precision.HIGHEST)


def _sigmoid(x):
    return 1.0 / (1.0 + jnp.exp(-x))


def _silu(x):
    return x * _sigmoid(x)


def _softplus(x):
    return jnp.maximum(x, 0.0) + jnp.log(1.0 + jnp.exp(-jnp.abs(x)))


def _rms(x):
    return x * lax.rsqrt(jnp.mean(x * x, axis=-1, keepdims=True) + EPS)


def _norm_mod(x, g, sc, sh):
    return _rms(x) * g * (1.0 + sc) + sh


def _ada_kernel(c_ref, w_ref, b_ref, o_ref):
    cs = _silu(c_ref[...]).astype(BF16)
    o_ref[0] = _dot(cs, w_ref[0].astype(BF16)) + b_ref[0]


def _ada(c_all, ada_w, ada_b):
    nb = c_all.shape[0]
    ncol = ada_w.shape[-1] // D_MODEL
    return pl.pallas_call(
        _ada_kernel,
        grid=(DEPTH, ncol),
        in_specs=[
            pl.BlockSpec((nb, D_MODEL), lambda l, j: (0, 0)),
            pl.BlockSpec((1, D_MODEL, D_MODEL), lambda l, j: (l, 0, j)),
            pl.BlockSpec((1, 1, D_MODEL), lambda l, j: (l, 0, j)),
        ],
        out_specs=pl.BlockSpec((1, nb, D_MODEL), lambda l, j: (l, 0, j)),
        out_shape=jax.ShapeDtypeStruct((DEPTH, nb, ncol * D_MODEL), F32),
        compiler_params=_cparams(("arbitrary", "arbitrary")),
        name="ada_mod",
    )(c_all, ada_w, ada_b.reshape(DEPTH, 1, -1))


def _row_specs(tm):
    x_spec = pl.BlockSpec((1, tm, D_MODEL), lambda b, t: (b, t, 0))
    mod_spec = pl.BlockSpec((1, 6, D_MODEL), lambda b, t: (b, 0, 0))
    return x_spec, mod_spec


def _mixer_in(x_ref, mod_ref, g_ref):
    return _norm_mod(x_ref[0], g_ref[...], mod_ref[0, 1:2, :], mod_ref[0, 0:1, :]).astype(BF16)


def _qkv_kernel(x_ref, mod_ref, g_ref, w_ref, q_ref, k_ref, v_ref, kb_ref, vb_ref):
    hm = _mixer_in(x_ref, mod_ref, g_ref)
    q_ref[0] = (_dot(hm, w_ref[:, 0:D_MODEL]) * (SB_HEAD_DIM ** -0.5)).astype(BF16)
    k = _dot(hm, w_ref[:, D_MODEL:2 * D_MODEL])
    k_ref[0] = k
    kb_ref[0] = k.astype(BF16)
    v = _dot(hm, w_ref[:, 2 * D_MODEL:3 * D_MODEL])
    v_ref[0] = v
    vb_ref[0] = v.astype(BF16)


def _qkv_proj(x, mod, g, w):
    B, T, _ = x.shape
    tm = min(ROW_TILE, T)
    x_spec, mod_spec = _row_specs(tm)
    out = lambda dt: jax.ShapeDtypeStruct((B, T, D_MODEL), dt)
    return pl.pallas_call(
        _qkv_kernel,
        grid=(B, T // tm),
        in_specs=[x_spec, mod_spec, _resident((1, D_MODEL)), _resident((D_MODEL, 3 * D_MODEL))],
        out_specs=[x_spec] * 5,
        out_shape=[out(BF16), out(F32), out(F32), out(BF16), out(BF16)],
        compiler_params=_cparams(("parallel", "parallel")),
        name="sb_qkv",
    )(x, mod, g, w)


def _gdn_in_kernel(x_ref, mod_ref, g_ref, w_ref, wab_ref, qkv_ref, z_ref, ab_ref):
    hm = _mixer_in(x_ref, mod_ref, g_ref)
    for j in range(3):
        cols = slice(j * D_MODEL, (j + 1) * D_MODEL)
        qkv_ref[0, :, cols] = _dot(hm, w_ref[:, cols])
    z_ref[0] = _dot(hm, w_ref[:, 3 * D_MODEL:4 * D_MODEL])
    ab_ref[0] = _dot(hm, wab_ref[...])


def _gdn_in_proj(x, mod, g, w, wab):
    B, T, _ = x.shape
    tm = min(ROW_TILE, T)
    x_spec, mod_spec = _row_specs(tm)
    return pl.pallas_call(
        _gdn_in_kernel,
        grid=(B, T // tm),
        in_specs=[x_spec, mod_spec, _resident((1, D_MODEL)), _resident((D_MODEL, 4 * D_MODEL)),
                  _resident((D_MODEL, LANES))],
        out_specs=[pl.BlockSpec((1, tm, 3 * D_MODEL), lambda b, t: (b, t, 0)), x_spec,
                   pl.BlockSpec((1, tm, LANES), lambda b, t: (b, t, 0))],
        out_shape=[jax.ShapeDtypeStruct((B, T, 3 * D_MODEL), F32), jax.ShapeDtypeStruct((B, T, D_MODEL), F32),
                   jax.ShapeDtypeStruct((B, T, LANES), F32)],
        compiler_params=_cparams(("parallel", "parallel")),
        name="gdn_in",
    )(x, mod, g, w, wab)


def _pw1_kernel(x_ref, mod_ref, g_ref, w_ref, u_ref):
    hm = _mixer_in(x_ref, mod_ref, g_ref)
    u_ref[0] = _dot(hm, w_ref[:, 0:D_MODEL]) * _sigmoid(_dot(hm, w_ref[:, D_MODEL:2 * D_MODEL]))


def _pw1_proj(x, mod, g, w):
    B, T, _ = x.shape
    tm = min(ROW_TILE, T)
    x_spec, mod_spec = _row_specs(tm)
    return pl.pallas_call(
        _pw1_kernel,
        grid=(B, T // tm),
        in_specs=[x_spec, mod_spec, _resident((1, D_MODEL)), _resident((D_MODEL, 2 * D_MODEL))],
        out_specs=x_spec,
        out_shape=jax.ShapeDtypeStruct((B, T, D_MODEL), F32),
        compiler_params=_cparams(("parallel", "parallel")),
        name="cnv_pw1",
    )(x, mod, g, w)


def _tail_kernel(x_ref, y_ref, mod_ref, wpre_ref, g_ref, w1_ref, w2_ref, gout_ref, o_ref, acc_ref, *, glu, final):
    pre = _dot(y_ref[0], wpre_ref[...])
    if glu:
        pre = pre[:, 0:D_MODEL] * _sigmoid(pre[:, D_MODEL:2 * D_MODEL])
    x1 = x_ref[0] + mod_ref[0, 2:3, :] * pre
    o_ref[0] = x1
    hf = _norm_mod(x1, g_ref[...], mod_ref[0, 4:5, :], mod_ref[0, 3:4, :]).astype(BF16)
    acc_ref[...] = jnp.zeros_like(acc_ref)

    def ff_step(k, carry):
        a = jnp.maximum(_dot(hf, w1_ref[k]), 0.0)
        acc_ref[...] += _dot((a * a).astype(BF16), w2_ref[k])
        return carry

    lax.fori_loop(0, D_FF // FF_CHUNK, ff_step, 0)
    x2 = o_ref[0] + mod_ref[0, 5:6, :] * acc_ref[...]
    if final:
        x2 = _rms(x2) * gout_ref[...]
    o_ref[0] = x2


def _tail(x, y, mod, w_pre, g_ffn, w1c, w2c, g_out, *, glu, final):
    B, T, _ = x.shape
    tm = min(ROW_TILE, T)
    x_spec, mod_spec = _row_specs(tm)
    ky = y.shape[-1]
    nff = D_FF // FF_CHUNK
    return pl.pallas_call(
        functools.partial(_tail_kernel, glu=glu, final=final),
        grid=(B, T // tm),
        in_specs=[x_spec, pl.BlockSpec((1, tm, ky), lambda b, t: (b, t, 0)), mod_spec,
                  _resident(w_pre.shape), _resident((1, D_MODEL)),
                  _resident((nff, D_MODEL, FF_CHUNK)), _resident((nff, FF_CHUNK, D_MODEL)),
                  _resident((1, D_MODEL))],
        out_specs=x_spec,
        out_shape=jax.ShapeDtypeStruct((B, T, D_MODEL), F32),
        scratch_shapes=[pltpu.VMEM((tm, D_MODEL), F32)],
        compiler_params=_cparams(("parallel", "parallel")),
        name="tail",
    )(x, y, mod, w_pre, g_ffn, w1c, w2c, g_out)


def _make_attn_kernel(tq, past, bq, bkn, bkp):
    nq = tq // bq
    ndiag = bq // bkn
    npast = past // bkp
    hd = SB_HEAD_DIM
    heads = tuple(slice(h * hd, (h + 1) * hd) for h in range(LANES // hd))

    def suffix_matrix(n):
        r = lax.broadcasted_iota(jnp.int32, (n, n), 0)
        c = lax.broadcasted_iota(jnp.int32, (n, n), 1)
        return (r >= c).astype(BF16)

    def kernel(*refs):
        if past:
            q_ref, k_ref, v_ref, kp_ref, vp_ref, o_ref = refs
        else:
            q_ref, k_ref, v_ref, o_ref = refs
        m_new = suffix_matrix(bkn)
        m_past = suffix_matrix(bkp) if past else None

        def block(qs, ks, vs, carries, accs, m, mask):
            zs = [_dot_nt(q, k) for q, k in zip(qs, ks)]
            sps = [_softplus(z) for z in zs]
            if mask is not None:
                sps = [jnp.where(mask, sp, 0.0) for sp in sps]
            his = [sp.astype(BF16) for sp in sps]
            los = [(sp - hi.astype(F32)).astype(BF16) for sp, hi in zip(sps, his)]
            sufs = [_dot(hi, m) + _dot(lo, m) for hi, lo in zip(his, los)]
            ws = [jnp.exp(z - suf - c) for z, suf, c in zip(zs, sufs, carries)]
            if mask is not None:
                ws = [jnp.where(mask, w, 0.0) for w in ws]
            accs = [acc + _dot(w.astype(BF16), v) for acc, w, v in zip(accs, ws, vs)]
            carries = [c + suf[:, 0:1] for c, suf in zip(carries, sufs)]
            return carries, accs

        def live(carries):
            lowest = carries[0]
            for c in carries[1:]:
                lowest = jnp.minimum(lowest, c)
            return jnp.min(lowest) < SB_SKIP

        def q_block(i, _):
            r0 = pl.multiple_of(i * bq, bq)
            qs = [q_ref[0, pl.ds(r0, bq), ls] for ls in heads]
            carries = [jnp.zeros((bq, 1), F32) for _ in heads]
            accs = [jnp.zeros((bq, hd), F32) for _ in heads]

            for d in reversed(range(ndiag)):
                rs = d * bkn
                k0 = pl.multiple_of(r0 + rs, bkn)
                visible = (lax.broadcasted_iota(jnp.int32, (bq - rs, bkn), 1)
                           < lax.broadcasted_iota(jnp.int32, (bq - rs, bkn), 0))
                c_sub, a_sub = block([q[rs:] for q in qs],
                                     [k_ref[0, pl.ds(k0, bkn), ls] for ls in heads],
                                     [v_ref[0, pl.ds(k0, bkn), ls] for ls in heads],
                                     [c[rs:] for c in carries], [a[rs:] for a in accs], m_new, visible)
                if rs:
                    c_sub = [jnp.concatenate([c[:rs], cs], axis=0) for c, cs in zip(carries, c_sub)]
                    a_sub = [jnp.concatenate([a[:rs], as_], axis=0) for a, as_ in zip(accs, a_sub)]
                carries, accs = c_sub, a_sub

            def sweep(n_blocks, kref, vref, bk, m, carries, accs):
                def cond(st):
                    return jnp.logical_and(st[0] < n_blocks, live(st[1]))

                def body(st):
                    jj, cs, as_ = st
                    k0 = pl.multiple_of((n_blocks - 1 - jj) * bk, bk)
                    cs, as_ = block(qs, [kref[0, pl.ds(k0, bk), ls] for ls in heads],
                                    [vref[0, pl.ds(k0, bk), ls] for ls in heads], list(cs), list(as_), m, None)
                    return jj + 1, tuple(cs), tuple(as_)

                _, cs, as_ = lax.while_loop(cond, body, (jnp.int32(0), tuple(carries), tuple(accs)))
                return list(cs), list(as_)

            carries, accs = sweep(i * ndiag, k_ref, v_ref, bkn, m_new, carries, accs)
            if past:
                carries, accs = sweep(npast, kp_ref, vp_ref, bkp, m_past, carries, accs)
            for h, ls in enumerate(heads):
                o_ref[0, pl.ds(r0, bq), ls] = accs[h].astype(BF16)
            return 0

        lax.fori_loop(0, nq, q_block, 0)

    return kernel


def _sb_attention(q, k, v, k_past=None, v_past=None):
    B, T, _ = q.shape
    past = 0 if k_past is None else k_past.shape[1]
    bq = min(SB_Q_ROWS, T)
    bkn = min(SB_KEY_BLOCK, T)
    seq_spec = pl.BlockSpec((1, T, LANES), lambda b, h: (b, 0, h))
    in_specs = [seq_spec] * 3
    args = [q, k, v]
    if past:
        in_specs += [pl.BlockSpec((1, past, LANES), lambda b, h: (b, 0, h))] * 2
        args += [k_past, v_past]
    return pl.pallas_call(
        _make_attn_kernel(T, past, bq, bkn, SB_KEY_BLOCK),
        grid=(B, D_MODEL // LANES),
        in_specs=in_specs,
        out_specs=seq_spec,
        out_shape=jax.ShapeDtypeStruct((B, T, D_MODEL), BF16),
        compiler_params=_cparams(("parallel", "parallel")),
        name="sb_attn",
    )(*args)


def _gelu_tanh(x):
    return 0.5 * x * (1.0 + jnp.tanh(0.7978845608028654 * (x + 0.044715 * (x * x * x))))


def _make_s5_kernel(c, bb, lane_chunk):
    rows = c * bb
    tile_lanes = S5_GROUPS_PER_TILE * S5_STATE
    tile_cols = S5_GROUPS_PER_TILE * S5_GROUP

    def kernel(x_ref, sh_ref, sc_ref, g_ref, wre_ref, wim_ref, are_ref, aim_ref, cw_ref, d_ref,
               s0r_ref, s0i_ref, y_ref, sr_ref, si_ref, hm_ref, xr_ref, xi_ref):
        @pl.when(pl.program_id(1) == 0)
        def _():
            sr_ref[...] = s0r_ref[...]
            si_ref[...] = s0i_ref[...]

        hm = _norm_mod(x_ref[...], g_ref[...], sc_ref[...][None], sh_ref[...][None])
        hm_ref[...] = hm.reshape(rows, D_MODEL)
        for kt in range(S5_TILES):
            hb = hm_ref[:, kt * tile_cols:(kt + 1) * tile_cols].astype(BF16)
            xr_ref[:, kt * tile_lanes:(kt + 1) * tile_lanes] = _dot(hb, wre_ref[kt])
            xi_ref[:, kt * tile_lanes:(kt + 1) * tile_lanes] = _dot(hb, wim_ref[kt])

        for lc in range(S5_LANES // lane_chunk):
            ls = slice(lc * lane_chunk, (lc + 1) * lane_chunk)
            a_r = are_ref[:, ls]
            a_i = aim_ref[:, ls]

            def step(t, carry):
                s_r, s_i = carry
                r0 = pl.multiple_of(t * bb, bb)
                n_r = a_r * s_r - a_i * s_i + xr_ref[pl.ds(r0, bb), ls]
                n_i = a_r * s_i + a_i * s_r + xi_ref[pl.ds(r0, bb), ls]
                xr_ref[pl.ds(r0, bb), ls] = n_r
                xi_ref[pl.ds(r0, bb), ls] = n_i
                return n_r, n_i

            s_r, s_i = lax.fori_loop(0, c, step, (sr_ref[:, ls], si_ref[:, ls]))
            sr_ref[:, ls] = s_r
            si_ref[:, ls] = s_i

        for nt in range(S5_TILES):
            ls = slice(nt * tile_lanes, (nt + 1) * tile_lanes)
            cs = slice(nt * tile_cols, (nt + 1) * tile_cols)
            st = jnp.concatenate([xr_ref[:, ls].astype(BF16), xi_ref[:, ls].astype(BF16)], axis=1)
            yv = _dot(st, cw_ref[nt]) + d_ref[:, cs] * hm_ref[:, cs]
            y_ref[:, :, cs] = _gelu_tanh(yv).astype(BF16).reshape(c, bb, tile_cols)

    return kernel


def _s5_scan(xt, sh, sc, g, wre, wim, a_re, a_im, cw, d, s0r, s0i):
    T, B, _ = xt.shape
    bb = min(B, 16)
    c = min(32, T)
    x_spec = pl.BlockSpec((c, bb, D_MODEL), lambda b, t: (t, b, 0))
    b_spec = pl.BlockSpec((bb, D_MODEL), lambda b, t: (b, 0))
    s_spec = pl.BlockSpec((bb, S5_LANES), lambda b, t: (b, 0))
    tile_lanes = S5_GROUPS_PER_TILE * S5_STATE
    tile_cols = S5_GROUPS_PER_TILE * S5_GROUP
    return pl.pallas_call(
        _make_s5_kernel(c, bb, 512),
        grid=(B // bb, T // c),
        in_specs=[x_spec, b_spec, b_spec, _resident((1, D_MODEL)),
                  _resident((S5_TILES, tile_cols, tile_lanes)), _resident((S5_TILES, tile_cols, tile_lanes)),
                  _resident((1, S5_LANES)), _resident((1, S5_LANES)),
                  _resident((S5_TILES, 2 * tile_lanes, tile_cols)), _resident((1, D_MODEL)),
                  s_spec, s_spec],
        out_specs=[x_spec, s_spec, s_spec],
        out_shape=[jax.ShapeDtypeStruct((T, B, D_MODEL), BF16), jax.ShapeDtypeStruct((B, S5_LANES), F32),
                   jax.ShapeDtypeStruct((B, S5_LANES), F32)],
        scratch_shapes=[pltpu.VMEM((c * bb, D_MODEL), F32), pltpu.VMEM((c * bb, S5_LANES), F32),
                        pltpu.VMEM((c * bb, S5_LANES), F32)],
        compiler_params=_cparams(("parallel", "arbitrary")),
        name="s5_scan",
    )(xt, sh, sc, g, wre, wim, a_re, a_im, cw, d, s0r, s0i)


def _s5_params(a_re, a_im, log_dt, b_re, b_im, c_re, c_im):
    dt = jnp.exp(log_dt.astype(F32))[:, None]
    ar, ai = a_re.astype(F32), a_im.astype(F32)
    mag = jnp.exp(ar * dt)
    abar_re, abar_im = mag * jnp.cos(ai * dt), mag * jnp.sin(ai * dt)
    den = ar * ar + ai * ai
    f_re = ((abar_re - 1.0) * ar + abar_im * ai) / den
    f_im = (abar_im * ar - (abar_re - 1.0) * ai) / den
    br, bi = b_re.astype(F32), b_im.astype(F32)
    bb_re = f_re[..., None] * br - f_im[..., None] * bi
    bb_im = f_re[..., None] * bi + f_im[..., None] * br
    gt = S5_GROUPS_PER_TILE
    eye = jnp.eye(gt, dtype=F32)

    def pack_in(w):
        w = w.reshape(S5_TILES, gt, S5_STATE, S5_GROUP)
        return jnp.einsum('kgpn,gh->kgnhp', w, eye).reshape(S5_TILES, gt * S5_GROUP, gt * S5_STATE).astype(BF16)

    def pack_out(w):
        w = w.reshape(S5_TILES, gt, S5_GROUP, S5_STATE)
        return jnp.einsum('kgnp,gh->kgphn', w, eye).reshape(S5_TILES, gt * S5_STATE, gt * S5_GROUP)

    cw = jnp.concatenate([pack_out(c_re.astype(F32)), -pack_out(c_im.astype(F32))], axis=1).astype(BF16)
    return (pack_in(bb_re), pack_in(bb_im), abar_re.reshape(1, S5_LANES), abar_im.reshape(1, S5_LANES), cw)


def _make_gdn_kernel(T, c, hb, cu):
    n = T // c
    span = cu * c
    dh = GDN_HEAD_DIM
    ng = c // SUBLANES
    pad0 = SUBLANES
    taps = GDN_CONV
    lanes = tuple(slice(j * dh, (j + 1) * dh) for j in range(hb))
    blk = min(GDN_SOLVE_BLOCK, c)

    def kernel(q_ref, k_ref, v_ref, z_ref, ab_ref, cq_ref, ck_ref, cv_ref, wq_ref, wk_ref, wv_ref,
               alog_ref, dtb_ref, ng_ref, s0_ref, o_ref, s_ref, pq_ref, pk_ref, pv_ref):
        h0 = pl.program_id(1) * hb
        s_ref[0] = s0_ref[0]
        for pref, cref in ((pq_ref, cq_ref), (pk_ref, ck_ref), (pv_ref, cv_ref)):
            pref[0:pad0, :] = jnp.zeros((pad0, hb * dh), F32)
            pref[pad0 - (taps - 1):pad0, :] = cref[0]

        r = lax.broadcasted_iota(jnp.int32, (c, c), 0)
        col = lax.broadcasted_iota(jnp.int32, (c, c), 1)
        incl = r >= col
        strict = r > col
        eye = r == col
        l_incl = incl.astype(F32)
        lane = lax.broadcasted_iota(jnp.int32, (1, LANES), 1)
        neg_decay_rate = -jnp.exp(alog_ref[...])

        def conv(src_ref, pref, w_ref, t0):
            pref[pad0:pad0 + span, :] = src_ref[0, pl.ds(t0, span), :]
            acc = w_ref[0:1, :] * pref[pl.ds(pad0 - (taps - 1), span), :]
            for i in range(1, taps):
                acc = acc + w_ref[i:i + 1, :] * pref[pl.ds(pad0 - (taps - 1) + i, span), :]
            pref[0:pad0, :] = pref[span:span + pad0, :]
            return _silu(acc)

        def pick(sel, m):
            return jnp.sum(jnp.where(sel, m, 0.0), axis=-1, keepdims=True)

        def l2n(x):
            return x * lax.rsqrt(jnp.sum(x * x, axis=-1, keepdims=True) + EPS)

        def chunk(ci, carry):
            t0 = pl.multiple_of(ci * span, span)
            pairs = [(u, j) for u in range(cu) for j in range(hb)]
            hs = range(len(pairs))
            q_all = conv(q_ref, pq_ref, wq_ref, t0)
            k_all = conv(k_ref, pk_ref, wk_ref, t0)
            v_all = conv(v_ref, pv_ref, wv_ref, t0)
            ab = ab_ref[0, pl.ds(t0, span), :]
            g_all = neg_decay_rate * _softplus(ab + dtb_ref[...])
            gcum_all = [_dot_f32(l_incl, g_all[u * c:(u + 1) * c]) for u in range(cu)]
            beta_all = _sigmoid(ab)

            def rows(u):
                return slice(u * c, (u + 1) * c)

            q = [l2n(q_all[rows(u), lanes[j]]) * (dh ** -0.5) for u, j in pairs]
            k = [l2n(k_all[rows(u), lanes[j]]) for u, j in pairs]
            v = [v_all[rows(u), lanes[j]] for u, j in pairs]
            gcum = [pick(lane == h0 + j, gcum_all[u]) for u, j in pairs]
            beta = [pick(lane == h0 + j + GDN_HEADS, beta_all[rows(u)]) for u, j in pairs]
            gcum_row = [jnp.sum(jnp.where(eye, g, 0.0), axis=0, keepdims=True) for g in gcum]
            decay = [jnp.where(incl, jnp.exp(jnp.where(incl, g - gr, 0.0)), 0.0) for g, gr in zip(gcum, gcum_row)]
            kb = [x.astype(BF16) for x in k]
            kk = [_dot_nt(x, x) for x in kb]
            qk = [_dot_nt(x.astype(BF16), y) for x, y in zip(q, kb)]
            a = [jnp.where(strict, b * m * d, 0.0) for b, m, d in zip(beta, kk, decay)]

            inv = [[jnp.where(eye[g * SUBLANES:(g + 1) * SUBLANES], 1.0, 0.0).astype(F32) for g in range(ng)]
                   for _ in hs]
            a_g = [[m[g * SUBLANES:(g + 1) * SUBLANES] for g in range(ng)] for m in a]
            gpb = blk // SUBLANES
            for step in range(blk - 1):
                for b in range(c // blk):
                    g0, jr = divmod(b * blk + step, SUBLANES)
                    for j in hs:
                        row = inv[j][g0][jr:jr + 1, :]
                        for g in range(g0, (b + 1) * gpb):
                            inv[j][g] = inv[j][g] - a_g[j][g][:, b * blk + step:b * blk + step + 1] * row
            inv = [jnp.concatenate(m, axis=0) for m in inv]
            size = blk
            while size < c:
                sh = size.bit_length() - 1
                off_mask = jnp.logical_and(r >> (sh + 1) == col >> (sh + 1), r >> sh != col >> sh)
                off = [jnp.where(off_mask, m, 0.0).astype(BF16) for m in a]
                inv_b = [m.astype(BF16) for m in inv]
                t = [_dot(off[j], inv_b[j]).astype(BF16) for j in hs]
                inv = [inv[j] - _dot(inv_b[j], t[j]) for j in hs]
                size *= 2

            eg = [jnp.exp(g) for g in gcum]
            inv_b = [m.astype(BF16) for m in inv]
            rhs = [jnp.concatenate([beta[j] * v[j], (beta[j] * eg[j]) * k[j]], axis=1) for j in hs]
            rhs_hi = [m.astype(BF16) for m in rhs]
            rhs_lo = [(m - mh.astype(F32)).astype(BF16) for m, mh in zip(rhs, rhs_hi)]
            sol = [_dot(inv_b[j], rhs_hi[j]) + _dot(inv_b[j], rhs_lo[j]) for j in hs]
            wq = [jnp.concatenate([sol[j][:, dh:2 * dh].astype(BF16), (q[j] * eg[j]).astype(BF16)], axis=0) for j in hs]
            qkd = [(qk[j] * decay[j]).astype(BF16) for j in hs]
            g_last = [gcum[j][c - 1:c, :] for j in hs]
            kd = [(k[j] * jnp.exp(g_last[j] - gcum[j])).astype(BF16) for j in hs]

            for u in range(cu):
                tu = pl.multiple_of(t0 + u * c, c)
                ids = [u * hb + j for j in range(hb)]
                s = [s_ref[0, j] for j in range(hb)]
                sb = [x.astype(BF16) for x in s]
                ws = [_dot(wq[i], sb[j]) for j, i in enumerate(ids)]
                vb = [(sol[i][:, 0:dh] - ws[j][0:c]).astype(BF16) for j, i in enumerate(ids)]
                o = [ws[j][c:2 * c] + _dot(qkd[i], vb[j]) for j, i in enumerate(ids)]
                for j, i in enumerate(ids):
                    s_ref[0, j] = s[j] * jnp.exp(g_last[i]) + lax.dot_general(
                        kd[i], vb[j], (((0,), (0,)), ((), ())), preferred_element_type=F32)
                for j, ls in enumerate(lanes):
                    y = _rms(o[j]) * ng_ref[...] * _silu(z_ref[0, pl.ds(tu, c), ls])
                    o_ref[0, pl.ds(tu, c), ls] = y.astype(BF16)
            return carry

        lax.fori_loop(0, n // cu, chunk, 0)

    return kernel


def _gdn_core(qkv, z, ab, cbuf, conv_w, alog_row, dtb_row, norm_g, s0):
    B, T, _ = qkv.shape
    c = min(GDN_CHUNK, T)
    H, dh = GDN_HEADS, GDN_HEAD_DIM
    hb = GDN_HEADS_PER_STEP
    G = H // hb

    def seq(off):
        return pl.BlockSpec((1, T, hb * dh), lambda b, h: (b, 0, off + h))

    def hist(off):
        return pl.BlockSpec((1, GDN_CONV - 1, hb * dh), lambda b, h: (b, 0, off + h))

    def wconv(off):
        return pl.BlockSpec((GDN_CONV, hb * dh), lambda b, h: (0, off + h))

    st_spec = pl.BlockSpec((1, hb, dh, dh), lambda b, h: (b, h, 0, 0))
    cu = GDN_CHUNKS_PER_STEP if (T // c) % GDN_CHUNKS_PER_STEP == 0 else 1
    pad = pltpu.VMEM((cu * c + SUBLANES, hb * dh), F32)
    return pl.pallas_call(
        _make_gdn_kernel(T, c, hb, cu),
        grid=(B, G),
        in_specs=[seq(0), seq(G), seq(2 * G), seq(0),
                  pl.BlockSpec((1, T, LANES), lambda b, h: (b, 0, 0)),
                  hist(0), hist(G), hist(2 * G), wconv(0), wconv(G), wconv(2 * G),
                  _resident((1, LANES)), _resident((1, LANES)), _resident((1, dh)), st_spec],
        out_specs=[seq(0), st_spec],
        out_shape=[jax.ShapeDtypeStruct((B, T, D_MODEL), BF16), jax.ShapeDtypeStruct((B, H, dh, dh), F32)],
        scratch_shapes=[pad, pad, pad],
        compiler_params=_cparams(("parallel", "parallel")),
        name="gdn_core",
    )(qkv, qkv, qkv, z, ab, cbuf, cbuf, cbuf, conv_w, conv_w, conv_w, alog_row, dtb_row, norm_g, s0)


def _make_conv_kernel(T, tq, rsub, lchunk):
    halo = CNV_HALO
    first_hist = halo - (CNV_WIDTH - 1)

    def kernel(u_ref, cb_ref, w_ref, b_ref, lg_ref, lb_ref, y_ref, uc_ref, acc_ref):
        t = pl.program_id(1)
        t0 = pl.multiple_of(t * tq, tq)

        @pl.when(t == 0)
        def _():
            uc_ref[0:halo, :] = jnp.zeros((halo, D_MODEL), F32)
            uc_ref[first_hist:halo, :] = cb_ref[0]

        @pl.when(t > 0)
        def _():
            uc_ref[0:halo, :] = u_ref[0, pl.ds(t0 - halo, halo), :]

        uc_ref[halo:halo + tq, :] = u_ref[0, pl.ds(t0, tq), :]

        for lc in range(D_MODEL // lchunk):
            ls = slice(lc * lchunk, (lc + 1) * lchunk)

            for r0 in range(0, tq, rsub):
                acc = w_ref[0:1, ls] * uc_ref[r0 + first_hist:r0 + first_hist + rsub, ls]
                for i in range(1, CNV_WIDTH):
                    acc = acc + w_ref[i:i + 1, ls] * uc_ref[r0 + first_hist + i:r0 + first_hist + i + rsub, ls]
                acc_ref[r0:r0 + rsub, ls] = acc + b_ref[:, ls]

        y = acc_ref[...]
        yc = y - jnp.mean(y, axis=-1, keepdims=True)
        var = jnp.mean(yc * yc, axis=-1, keepdims=True)
        y_ref[0] = _silu(yc * lax.rsqrt(var + EPS) * lg_ref[...] + lb_ref[...]).astype(BF16)

    return kernel


def _conv_core(u, cbuf, dw_w, dw_b, ln_g, ln_b):
    B, T, _ = u.shape
    tq = min(256, T)
    rsub = min(64, tq)
    row = _resident((1, D_MODEL))
    return pl.pallas_call(
        _make_conv_kernel(T, tq, rsub, 256),
        grid=(B, T // tq),
        in_specs=[pl.BlockSpec((1, T, D_MODEL), lambda b, t: (b, 0, 0)),
                  pl.BlockSpec((1, CNV_WIDTH - 1, D_MODEL), lambda b, t: (b, 0, 0)),
                  _resident((CNV_WIDTH, D_MODEL)), row, row, row],
        out_specs=pl.BlockSpec((1, tq, D_MODEL), lambda b, t: (b, t, 0)),
        out_shape=jax.ShapeDtypeStruct((B, T, D_MODEL), BF16),
        scratch_shapes=[pltpu.VMEM((tq + CNV_HALO, D_MODEL), F32), pltpu.VMEM((tq, D_MODEL), F32)],
        compiler_params=_cparams(("parallel", "arbitrary")),
        name="cnv_core",
    )(u, cbuf, dw_w, dw_b, ln_g, ln_b)


def _row(v):
    return v.astype(F32).reshape(1, -1)


def _pad_lanes(v):
    return jnp.pad(v.astype(F32), (0, LANES - v.shape[0])).reshape(1, LANES)


def _prep_weights(p):
    nff = D_FF // FF_CHUNK
    w = {}
    w['w1'] = [p['ffn_w1'][i].astype(BF16).reshape(D_MODEL, nff, FF_CHUNK).transpose(1, 0, 2) for i in range(DEPTH)]
    w['w2'] = [p['ffn_w2'][i].astype(BF16).reshape(nff, FF_CHUNK, D_MODEL) for i in range(DEPTH)]
    w['sb_qkv'] = p['sb_w_qkv'][0].astype(BF16)
    w['sb_o'] = p['sb_w_o'][0].astype(BF16)
    w['s5'] = _s5_params(p['s5_a_re'][0], p['s5_a_im'][0], p['s5_log_dt'][0], p['s5_b_re'][0], p['s5_b_im'][0],
                         p['s5_c_re'][0], p['s5_c_im'][0])
    w['s5_glu'] = p['s5_w_glu'][0].astype(BF16)
    w_in = p['gdn_w_in'][0]
    w['gdn_in'] = w_in[:, :4 * D_MODEL].astype(BF16)
    w['gdn_ab'] = jnp.pad(w_in[:, 4 * D_MODEL:], ((0, 0), (0, LANES - 2 * GDN_HEADS))).astype(BF16)
    w['gdn_o'] = p['gdn_w_o'][0].astype(BF16)
    w['pw1'] = p['cnv_w_pw1'][0].astype(BF16)
    w['pw2'] = p['cnv_w_pw2'][0].astype(BF16)
    return w


def _trunk(x, mod, sb_k, sb_v, s5_re, s5_im, gdn_s, gdn_cb, cnv_cb, p, w):
    B, T, _ = x.shape
    out = {}
    g_out = _row(p['norm_out_g'])

    def tail(i, x, y, w_pre, glu=False):
        return _tail(x, y, mod[i], w_pre, _row(p['norm_ffn_g'][i]), w['w1'][i], w['w2'][i], g_out,
                     glu=glu, final=(i == DEPTH - 1))

    q, k, v, kb, vb = _qkv_proj(x, mod[0], _row(p['norm_mix_g'][0]), w['sb_qkv'])
    if sb_k is None:
        o = _sb_attention(q, kb, vb)
    else:
        P = sb_k.shape[1]
        o = _sb_attention(q, kb, vb, sb_k.reshape(B, P, D_MODEL).astype(BF16),
                          sb_v.reshape(B, P, D_MODEL).astype(BF16))
    out['sb_k'] = k.reshape(1, B, T, SB_HEADS, SB_HEAD_DIM)
    out['sb_v'] = v.reshape(1, B, T, SB_HEADS, SB_HEAD_DIM)
    x = tail(0, x, o, w['sb_o'])

    wre, wim, a_re, a_im, cw = w['s5']
    yt, s_re, s_im = _s5_scan(x.transpose(1, 0, 2), mod[1][:, 0, :], mod[1][:, 1, :], _row(p['norm_mix_g'][1]),
                              wre, wim, a_re, a_im, cw, _row(p['s5_d'][0]),
                              s5_re.reshape(B, S5_LANES), s5_im.reshape(B, S5_LANES))
    out['s5_re'] = s_re.reshape(1, B, S5_GROUPS, S5_STATE)
    out['s5_im'] = s_im.reshape(1, B, S5_GROUPS, S5_STATE)
    x = tail(1, x, yt.transpose(1, 0, 2), w['s5_glu'], glu=True)

    qkv, z, ab = _gdn_in_proj(x, mod[2], _row(p['norm_mix_g'][2]), w['gdn_in'], w['gdn_ab'])
    o, s_new = _gdn_core(qkv, z, ab, gdn_cb, p['gdn_conv_w'][0].astype(F32), _pad_lanes(p['gdn_a_log'][0]),
                         _pad_lanes(p['gdn_dt_bias'][0]), _row(p['gdn_norm_g'][0]), gdn_s)
    out['gdn'] = s_new[None]
    hist = jnp.concatenate([gdn_cb, qkv], axis=1) if T < GDN_CONV - 1 else qkv
    out['gdn_conv'] = hist[None, :, hist.shape[1] - (GDN_CONV - 1):]
    x = tail(2, x, o, w['gdn_o'])

    u = _pw1_proj(x, mod[3], _row(p['norm_mix_g'][3]), w['pw1'])
    ya = _conv_core(u, cnv_cb, p['cnv_dw_w'][0].astype(F32), _row(p['cnv_dw_b'][0]), _row(p['cnv_ln_g'][0]),
                    _row(p['cnv_ln_b'][0]))
    hist = jnp.concatenate([cnv_cb, u], axis=1) if T < CNV_WIDTH - 1 else u
    out['conv'] = hist[None, :, hist.shape[1] - (CNV_WIDTH - 1):]
    x = tail(3, x, ya, w['pw2'])
    return x, out


def kernel(x_prompt, x_sample, c_prompt, c_sample, cache_sb_k, cache_sb_v, state_s5_re, state_s5_im, state_gdn, state_gdn_conv, state_conv, ada_w, ada_b, norm_mix_g, norm_ffn_g, norm_out_g, ffn_w1, ffn_w2, sb_w_qkv, sb_w_o, s5_a_re, s5_a_im, s5_log_dt, s5_b_re, s5_b_im, s5_c_re, s5_c_im, s5_d, s5_w_glu, gdn_w_in, gdn_conv_w, gdn_a_log, gdn_dt_bias, gdn_norm_g, gdn_w_o, cnv_w_pw1, cnv_dw_w, cnv_dw_b, cnv_ln_g, cnv_ln_b, cnv_w_pw2):
    p = dict(norm_mix_g=norm_mix_g, norm_ffn_g=norm_ffn_g, norm_out_g=norm_out_g, ffn_w1=ffn_w1, ffn_w2=ffn_w2,
             sb_w_qkv=sb_w_qkv, sb_w_o=sb_w_o, s5_a_re=s5_a_re, s5_a_im=s5_a_im, s5_log_dt=s5_log_dt,
             s5_b_re=s5_b_re, s5_b_im=s5_b_im, s5_c_re=s5_c_re, s5_c_im=s5_c_im, s5_d=s5_d, s5_w_glu=s5_w_glu,
             gdn_w_in=gdn_w_in, gdn_conv_w=gdn_conv_w, gdn_a_log=gdn_a_log, gdn_dt_bias=gdn_dt_bias,
             gdn_norm_g=gdn_norm_g, gdn_w_o=gdn_w_o, cnv_w_pw1=cnv_w_pw1, cnv_dw_w=cnv_dw_w, cnv_dw_b=cnv_dw_b,
             cnv_ln_g=cnv_ln_g, cnv_ln_b=cnv_ln_b, cnv_w_pw2=cnv_w_pw2)
    w = _prep_weights(p)
    bp, bs = x_prompt.shape[0], x_sample.shape[0]
    mod = _ada(jnp.concatenate([c_prompt, c_sample], axis=0), ada_w, ada_b)
    mod = mod.reshape(DEPTH, bp + bs, 6, D_MODEL)
    mod_p = [mod[i, :bp] for i in range(DEPTH)]
    mod_s = [mod[i, bp:] for i in range(DEPTH)]

    yp, op = _trunk(
        x_prompt, mod_p, None, None,
        jnp.zeros((bp, S5_GROUPS, S5_STATE), F32), jnp.zeros((bp, S5_GROUPS, S5_STATE), F32),
        jnp.zeros((bp, GDN_HEADS, GDN_HEAD_DIM, GDN_HEAD_DIM), F32),
        jnp.zeros((bp, GDN_CONV - 1, 3 * D_MODEL), F32), jnp.zeros((bp, CNV_WIDTH - 1, D_MODEL), F32), p, w)
    ys, os_ = _trunk(
        x_sample, mod_s, cache_sb_k[0], cache_sb_v[0], state_s5_re[0], state_s5_im[0],
        state_gdn[0], state_gdn_conv[0], state_conv[0], p, w)
    return (yp, ys, op['sb_k'], op['sb_v'], os_['sb_k'], os_['sb_v'],
            op['s5_re'], op['s5_im'], os_['s5_re'], os_['s5_im'],
            op['gdn'], op['gdn_conv'], os_['gdn'], os_['gdn_conv'],
            op['conv'], os_['conv'])
```

```python
import functools

import jax
import jax.numpy as jnp
from jax import lax
from jax.experimental import pallas as pl
from jax.experimental.pallas import tpu as pltpu

F32 = jnp.float32
BF16 = jnp.bfloat16

D_MODEL = 1024
DEPTH = 4
EPS = 1e-6
D_FF = 4 * D_MODEL

SB_HEADS = 16
SB_HEAD_DIM = 64
SB_Q_ROWS = 512
SB_KEY_BLOCK = 256
SB_SKIP = 104.0

S5_GROUPS = 64
S5_GROUP = 16
S5_STATE = 64
S5_LANES = S5_GROUPS * S5_STATE
S5_GROUPS_PER_TILE = 16
S5_TILES = S5_GROUPS // S5_GROUPS_PER_TILE

GDN_HEADS = 8
GDN_HEAD_DIM = 128
GDN_CONV = 4
GDN_CHUNK = 64
GDN_HEADS_PER_STEP = 4
GDN_CHUNKS_PER_STEP = 4
GDN_SOLVE_BLOCK = 16

CNV_WIDTH = 31
CNV_HALO = 32

LANES = 128
SUBLANES = 8
ROW_TILE = 512
FF_CHUNK = 512
FF_UNROLL = 4
V7X_VMEM_BYTES = 64 * 2 ** 20
VMEM_LIMIT = 56 * 2 ** 20


def _cparams(sem):
    return pltpu.CompilerParams(dimension_semantics=sem, vmem_limit_bytes=VMEM_LIMIT)


def _resident(shape):
    nd = len(shape)
    return pl.BlockSpec(shape, lambda *_: (0,) * nd, pipeline_mode=pl.Buffered(1))


def _dot(a, b):
    return jnp.dot(a, b, preferred_element_type=F32)


def _dot_nt(a, b):
    return lax.dot_general(a, b, (((1,), (1,)), ((), ())), preferred_element_type=F32)


def _dot_f32(a, b):
    return jnp.dot(a, b, preferred_element_type=F32, precision=lax.Precision.HIGHEST)


def _sigmoid(x):
    return 1.0 / (1.0 + jnp.exp(-x))


def _silu(x):
    return x * _sigmoid(x)


def _softplus(x):
    return jnp.maximum(x, 0.0) + jnp.log(1.0 + jnp.exp(-jnp.abs(x)))


def _rms(x):
    return x * lax.rsqrt(jnp.mean(x * x, axis=-1, keepdims=True) + EPS)


def _norm_mod(x, g, sc, sh):
    return _rms(x) * g * (1.0 + sc) + sh


def _ada_kernel(c_ref, w_ref, b_ref, o_ref):
    cs = _silu(c_ref[...]).astype(BF16)
    o_ref[0] = _dot(cs, w_ref[0].astype(BF16)) + b_ref[0]


def _ada(c_all, ada_w, ada_b):
    nb = c_all.shape[0]
    ncol = ada_w.shape[-1] // D_MODEL
    return pl.pallas_call(
        _ada_kernel,
        grid=(DEPTH, ncol),
        in_specs=[
            pl.BlockSpec((nb, D_MODEL), lambda l, j: (0, 0)),
            pl.BlockSpec((1, D_MODEL, D_MODEL), lambda l, j: (l, 0, j)),
            pl.BlockSpec((1, 1, D_MODEL), lambda l, j: (l, 0, j)),
        ],
        out_specs=pl.BlockSpec((1, nb, D_MODEL), lambda l, j: (l, 0, j)),
        out_shape=jax.ShapeDtypeStruct((DEPTH, nb, ncol * D_MODEL), F32),
        compiler_params=_cparams(("arbitrary", "arbitrary")),
        name="ada_mod",
    )(c_all, ada_w, ada_b.reshape(DEPTH, 1, -1))


def _row_specs(tm):
    x_spec = pl.BlockSpec((1, tm, D_MODEL), lambda b, t: (b, t, 0))
    mod_spec = pl.BlockSpec((1, 6, D_MODEL), lambda b, t: (b, 0, 0))
    return x_spec, mod_spec


def _mixer_in(x_ref, mod_ref, g_ref):
    return _norm_mod(x_ref[0], g_ref[...], mod_ref[0, 1:2, :], mod_ref[0, 0:1, :]).astype(BF16)


def _qkv_kernel(x_ref, mod_ref, g_ref, w_ref, q_ref, k_ref, v_ref, kb_ref, vb_ref):
    hm = _mixer_in(x_ref, mod_ref, g_ref)
    q_ref[0] = (_dot(hm, w_ref[:, 0:D_MODEL]) * (SB_HEAD_DIM ** -0.5)).astype(BF16)
    k = _dot(hm, w_ref[:, D_MODEL:2 * D_MODEL])
    k_ref[0] = k
    kb_ref[0] = k.astype(BF16)
    v = _dot(hm, w_ref[:, 2 * D_MODEL:3 * D_MODEL])
    v_ref[0] = v
    vb_ref[0] = v.astype(BF16)


def _qkv_proj(x, mod, g, w):
    B, T, _ = x.shape
    tm = min(ROW_TILE, T)
    x_spec, mod_spec = _row_specs(tm)
    out = lambda dt: jax.ShapeDtypeStruct((B, T, D_MODEL), dt)
    return pl.pallas_call(
        _qkv_kernel,
        grid=(B, T // tm),
        in_specs=[x_spec, mod_spec, _resident((1, D_MODEL)), _resident((D_MODEL, 3 * D_MODEL))],
        out_specs=[x_spec] * 5,
        out_shape=[out(BF16), out(F32), out(F32), out(BF16), out(BF16)],
        compiler_params=_cparams(("parallel", "parallel")),
        name="sb_qkv",
    )(x, mod, g, w)


def _gdn_in_kernel(x_ref, mod_ref, g_ref, w_ref, wab_ref, qkv_ref, z_ref, ab_ref):
    hm = _mixer_in(x_ref, mod_ref, g_ref)
    for j in range(3):
        cols = slice(j * D_MODEL, (j + 1) * D_MODEL)
        qkv_ref[0, :, cols] = _dot(hm, w_ref[:, cols])
    z_ref[0] = _dot(hm, w_ref[:, 3 * D_MODEL:4 * D_MODEL])
    ab_ref[0] = _dot(hm, wab_ref[...])


def _gdn_in_proj(x, mod, g, w, wab):
    B, T, _ = x.shape
    tm = min(ROW_TILE, T)
    x_spec, mod_spec = _row_specs(tm)
    return pl.pallas_call(
        _gdn_in_kernel,
        grid=(B, T // tm),
        in_specs=[x_spec, mod_spec, _resident((1, D_MODEL)), _resident((D_MODEL, 4 * D_MODEL)),
                  _resident((D_MODEL, LANES))],
        out_specs=[pl.BlockSpec((1, tm, 3 * D_MODEL), lambda b, t: (b, t, 0)), x_spec,
                   pl.BlockSpec((1, tm, LANES), lambda b, t: (b, t, 0))],
        out_shape=[jax.ShapeDtypeStruct((B, T, 3 * D_MODEL), F32), jax.ShapeDtypeStruct((B, T, D_MODEL), F32),
                   jax.ShapeDtypeStruct((B, T, LANES), F32)],
        compiler_params=_cparams(("parallel", "parallel")),
        name="gdn_in",
    )(x, mod, g, w, wab)


def _pw1_kernel(x_ref, mod_ref, g_ref, w_ref, u_ref):
    hm = _mixer_in(x_ref, mod_ref, g_ref)
    u_ref[0] = _dot(hm, w_ref[:, 0:D_MODEL]) * _sigmoid(_dot(hm, w_ref[:, D_MODEL:2 * D_MODEL]))


def _pw1_proj(x, mod, g, w):
    B, T, _ = x.shape
    tm = min(ROW_TILE, T)
    x_spec, mod_spec = _row_specs(tm)
    return pl.pallas_call(
        _pw1_kernel,
        grid=(B, T // tm),
        in_specs=[x_spec, mod_spec, _resident((1, D_MODEL)), _resident((D_MODEL, 2 * D_MODEL))],
        out_specs=x_spec,
        out_shape=jax.ShapeDtypeStruct((B, T, D_MODEL), F32),
        compiler_params=_cparams(("parallel", "parallel")),
        name="cnv_pw1",
    )(x, mod, g, w)


def _tail_kernel(x_ref, y_ref, mod_ref, wpre_ref, g_ref, w1_ref, w2_ref, gout_ref, o_ref, acc_ref, *, glu, final):
    pre = _dot(y_ref[0], wpre_ref[...])
    if glu:
        pre = pre[:, 0:D_MODEL] * _sigmoid(pre[:, D_MODEL:2 * D_MODEL])
    x1 = x_ref[0] + mod_ref[0, 2:3, :] * pre
    o_ref[0] = x1
    hf = _norm_mod(x1, g_ref[...], mod_ref[0, 4:5, :], mod_ref[0, 3:4, :]).astype(BF16)

    def ff_pair(k):
        a = [jnp.maximum(_dot(hf, w1_ref[k + u]), 0.0) for u in range(FF_UNROLL)]
        p = [_dot((a[u] * a[u]).astype(BF16), w2_ref[k + u]) for u in range(FF_UNROLL)]
        return functools.reduce(lambda s, t: s + t, p)

    acc_ref[...] = ff_pair(0)

    def ff_step(k, carry):
        acc_ref[...] += ff_pair(k * FF_UNROLL)
        return carry

    lax.fori_loop(1, D_FF // (FF_CHUNK * FF_UNROLL), ff_step, 0)
    x2 = o_ref[0] + mod_ref[0, 5:6, :] * acc_ref[...]
    if final:
        x2 = _rms(x2) * gout_ref[...]
    o_ref[0] = x2


def _tail(x, y, mod, w_pre, g_ffn, w1c, w2c, g_out, *, glu, final):
    B, T, _ = x.shape
    tm = min(ROW_TILE, T)
    x_spec, mod_spec = _row_specs(tm)
    ky = y.shape[-1]
    nff = D_FF // FF_CHUNK
    return pl.pallas_call(
        functools.partial(_tail_kernel, glu=glu, final=final),
        grid=(B, T // tm),
        in_specs=[x_spec, pl.BlockSpec((1, tm, ky), lambda b, t: (b, t, 0)), mod_spec,
                  _resident(w_pre.shape), _resident((1, D_MODEL)),
                  _resident((nff, D_MODEL, FF_CHUNK)), _resident((nff, FF_CHUNK, D_MODEL)),
                  _resident((1, D_MODEL))],
        out_specs=x_spec,
        out_shape=jax.ShapeDtypeStruct((B, T, D_MODEL), F32),
        scratch_shapes=[pltpu.VMEM((tm, D_MODEL), F32)],
        compiler_params=_cparams(("parallel", "parallel")),
        name="tail",
    )(x, y, mod, w_pre, g_ffn, w1c, w2c, g_out)


def _make_attn_kernel(tq, past, bq, bkn, bkp):
    nq = tq // bq
    ndiag = bq // bkn
    npast = past // bkp
    hd = SB_HEAD_DIM
    heads = tuple(slice(h * hd, (h + 1) * hd) for h in range(LANES // hd))

    def suffix_matrix(n):
        r = lax.broadcasted_iota(jnp.int32, (n, n), 0)
        c = lax.broadcasted_iota(jnp.int32, (n, n), 1)
        return (r >= c).astype(BF16)

    def kernel(*refs):
        if past:
            q_ref, k_ref, v_ref, kp_ref, vp_ref, o_ref = refs
        else:
            q_ref, k_ref, v_ref, o_ref = refs
        m_new = suffix_matrix(bkn)
        m_past = suffix_matrix(bkp) if past else None

        def scores(qs, ks, masks, m):
            zs = [_dot_nt(q, k) for q, k in zip(qs, ks)]
            sps = [_softplus(z) for z in zs]
            sps = [sp if mk is None else jnp.where(mk, sp, 0.0) for sp, mk in zip(sps, masks)]
            his = [sp.astype(BF16) for sp in sps]
            los = [(sp - hi.astype(F32)).astype(BF16) for sp, hi in zip(sps, his)]
            return zs, [_dot(hi, m) + _dot(lo, m) for hi, lo in zip(his, los)]

        def weighted(zs, sufs, carries, vs, masks):
            ws = [jnp.exp(z - suf - c) for z, suf, c in zip(zs, sufs, carries)]
            ws = [w if mk is None else jnp.where(mk, w, 0.0) for w, mk in zip(ws, masks)]
            return [_dot(w.astype(BF16), v) for w, v in zip(ws, vs)]

        def live(carries):
            lowest = carries[0]
            for c in carries[1:]:
                lowest = jnp.minimum(lowest, c)
            return jnp.min(lowest) < SB_SKIP

        def q_block(i, _):
            r0 = pl.multiple_of(i * bq, bq)
            qs = [q_ref[0, pl.ds(r0, bq), ls] for ls in heads]
            carries = [jnp.zeros((bq, 1), F32) for _ in heads]
            accs = [jnp.zeros((bq, hd), F32) for _ in heads]

            items = [(d * bkn, h) for d in reversed(range(ndiag)) for h in range(len(heads))]

            def inside(ref):
                return [ref[0, pl.ds(pl.multiple_of(r0 + rs, bkn), bkn), heads[h]] for rs, h in items]

            visible = [lax.broadcasted_iota(jnp.int32, (bq - rs, bkn), 1)
                       < lax.broadcasted_iota(jnp.int32, (bq - rs, bkn), 0) for rs, _ in items]
            zs, sufs = scores([qs[h][rs:] for rs, h in items], inside(k_ref), visible, m_new)
            newer = []
            for (rs, h), suf in zip(items, sufs):
                newer.append(carries[h][rs:])
                mass = suf[:, 0:1]
                if rs:
                    mass = jnp.concatenate([jnp.zeros((rs, 1), F32), mass], axis=0)
                carries[h] = carries[h] + mass
            for (rs, h), part in zip(items, weighted(zs, sufs, newer, inside(v_ref), visible)):
                if rs:
                    part = jnp.concatenate([jnp.zeros((rs, hd), F32), part], axis=0)
                accs[h] = accs[h] + part

            def sweep(n_blocks, kref, vref, bk, m, carries, accs):
                def cond(st):
                    return jnp.logical_and(st[0] < n_blocks, live(st[1]))

                def body(st):
                    jj, cs, as_ = st
                    k0 = pl.multiple_of((n_blocks - 1 - jj) * bk, bk)
                    unmasked = [None] * len(heads)
                    zs, sufs = scores(qs, [kref[0, pl.ds(k0, bk), ls] for ls in heads], unmasked, m)
                    parts = weighted(zs, sufs, cs, [vref[0, pl.ds(k0, bk), ls] for ls in heads], unmasked)
                    return (jj + 1, tuple(c + suf[:, 0:1] for c, suf in zip(cs, sufs)),
                            tuple(a + p for a, p in zip(as_, parts)))

                _, cs, as_ = lax.while_loop(cond, body, (jnp.int32(0), tuple(carries), tuple(accs)))
                return list(cs), list(as_)

            carries, accs = sweep(i * ndiag, k_ref, v_ref, bkn, m_new, carries, accs)
            if past:
                carries, accs = sweep(npast, kp_ref, vp_ref, bkp, m_past, carries, accs)
            for h, ls in enumerate(heads):
                o_ref[0, pl.ds(r0, bq), ls] = accs[h].astype(BF16)
            return 0

        lax.fori_loop(0, nq, q_block, 0)

    return kernel


def _sb_attention(q, k, v, k_past=None, v_past=None):
    B, T, _ = q.shape
    past = 0 if k_past is None else k_past.shape[1]
    bq = min(SB_Q_ROWS, T)
    bkn = min(SB_KEY_BLOCK, T)
    seq_spec = pl.BlockSpec((1, T, LANES), lambda b, h: (b, 0, h))
    in_specs = [seq_spec] * 3
    args = [q, k, v]
    if past:
        in_specs += [pl.BlockSpec((1, past, LANES), lambda b, h: (b, 0, h))] * 2
        args += [k_past, v_past]
    return pl.pallas_call(
        _make_attn_kernel(T, past, bq, bkn, SB_KEY_BLOCK),
        grid=(B, D_MODEL // LANES),
        in_specs=in_specs,
        out_specs=seq_spec,
        out_shape=jax.ShapeDtypeStruct((B, T, D_MODEL), BF16),
        compiler_params=_cparams(("parallel", "parallel")),
        name="sb_attn",
    )(*args)


def _gelu_tanh(x):
    return 0.5 * x * (1.0 + jnp.tanh(0.7978845608028654 * (x + 0.044715 * (x * x * x))))


def _make_s5_kernel(c, bb, lane_chunk):
    rows = c * bb
    tile_lanes = S5_GROUPS_PER_TILE * S5_STATE
    tile_cols = S5_GROUPS_PER_TILE * S5_GROUP

    def kernel(x_ref, sh_ref, sc_ref, g_ref, wre_ref, wim_ref, are_ref, aim_ref, cw_ref, d_ref,
               s0r_ref, s0i_ref, y_ref, sr_ref, si_ref, hm_ref, xr_ref, xi_ref):
        @pl.when(pl.program_id(1) == 0)
        def _():
            sr_ref[...] = s0r_ref[...]
            si_ref[...] = s0i_ref[...]

        xt = jnp.swapaxes(x_ref[...], 0, 1)
        hm = _norm_mod(xt, g_ref[...], sc_ref[...][None], sh_ref[...][None])
        hm_ref[...] = hm.reshape(rows, D_MODEL)
        for kt in range(S5_TILES):
            hb = hm_ref[:, kt * tile_cols:(kt + 1) * tile_cols].astype(BF16)
            xr_ref[:, kt * tile_lanes:(kt + 1) * tile_lanes] = _dot(hb, wre_ref[kt])
            xi_ref[:, kt * tile_lanes:(kt + 1) * tile_lanes] = _dot(hb, wim_ref[kt])

        for lc in range(S5_LANES // lane_chunk):
            ls = slice(lc * lane_chunk, (lc + 1) * lane_chunk)
            a_r = are_ref[:, ls]
            a_i = aim_ref[:, ls]

            def step(t, carry):
                s_r, s_i = carry
                r0 = pl.multiple_of(t * bb, bb)
                n_r = a_r * s_r - a_i * s_i + xr_ref[pl.ds(r0, bb), ls]
                n_i = a_r * s_i + a_i * s_r + xi_ref[pl.ds(r0, bb), ls]
                xr_ref[pl.ds(r0, bb), ls] = n_r
                xi_ref[pl.ds(r0, bb), ls] = n_i
                return n_r, n_i

            s_r, s_i = lax.fori_loop(0, c, step, (sr_ref[:, ls], si_ref[:, ls]), unroll=2)
            sr_ref[:, ls] = s_r
            si_ref[:, ls] = s_i

        for nt in range(S5_TILES):
            ls = slice(nt * tile_lanes, (nt + 1) * tile_lanes)
            cs = slice(nt * tile_cols, (nt + 1) * tile_cols)
            st = jnp.concatenate([xr_ref[:, ls].astype(BF16), xi_ref[:, ls].astype(BF16)], axis=1)
            yv = _dot(st, cw_ref[nt]) + d_ref[:, cs] * hm_ref[:, cs]
            y_ref[:, :, cs] = jnp.swapaxes(_gelu_tanh(yv).reshape(c, bb, tile_cols), 0, 1).astype(BF16)

    return kernel


def _s5_scan(x, sh, sc, g, wre, wim, a_re, a_im, cw, d, s0r, s0i):
    B, T, _ = x.shape
    bb = min(B, 16)
    c = min(32, T)
    x_spec = pl.BlockSpec((bb, c, D_MODEL), lambda b, t: (b, t, 0))
    b_spec = pl.BlockSpec((bb, D_MODEL), lambda b, t: (b, 0))
    s_spec = pl.BlockSpec((bb, S5_LANES), lambda b, t: (b, 0))
    tile_lanes = S5_GROUPS_PER_TILE * S5_STATE
    tile_cols = S5_GROUPS_PER_TILE * S5_GROUP
    return pl.pallas_call(
        _make_s5_kernel(c, bb, 512),
        grid=(B // bb, T // c),
        in_specs=[x_spec, b_spec, b_spec, _resident((1, D_MODEL)),
                  _resident((S5_TILES, tile_cols, tile_lanes)), _resident((S5_TILES, tile_cols, tile_lanes)),
                  _resident((1, S5_LANES)), _resident((1, S5_LANES)),
                  _resident((S5_TILES, 2 * tile_lanes, tile_cols)), _resident((1, D_MODEL)),
                  s_spec, s_spec],
        out_specs=[x_spec, s_spec, s_spec],
        out_shape=[jax.ShapeDtypeStruct((B, T, D_MODEL), BF16), jax.ShapeDtypeStruct((B, S5_LANES), F32),
                   jax.ShapeDtypeStruct((B, S5_LANES), F32)],
        scratch_shapes=[pltpu.VMEM((c * bb, D_MODEL), F32), pltpu.VMEM((c * bb, S5_LANES), F32),
                        pltpu.VMEM((c * bb, S5_LANES), F32)],
        compiler_params=_cparams(("parallel", "arbitrary")),
        name="s5_scan",
    )(x, sh, sc, g, wre, wim, a_re, a_im, cw, d, s0r, s0i)


def _s5_params(a_re, a_im, log_dt, b_re, b_im, c_re, c_im):
    dt = jnp.exp(log_dt.astype(F32))[:, None]
    ar, ai = a_re.astype(F32), a_im.astype(F32)
    mag = jnp.exp(ar * dt)
    abar_re, abar_im = mag * jnp.cos(ai * dt), mag * jnp.sin(ai * dt)
    den = ar * ar + ai * ai
    f_re = ((abar_re - 1.0) * ar + abar_im * ai) / den
    f_im = (abar_im * ar - (abar_re - 1.0) * ai) / den
    br, bi = b_re.astype(F32), b_im.astype(F32)
    bb_re = f_re[..., None] * br - f_im[..., None] * bi
    bb_im = f_re[..., None] * bi + f_im[..., None] * br
    gt = S5_GROUPS_PER_TILE
    eye = jnp.eye(gt, dtype=F32)

    def pack_in(w):
        w = w.reshape(S5_TILES, gt, S5_STATE, S5_GROUP)
        return jnp.einsum('kgpn,gh->kgnhp', w, eye).reshape(S5_TILES, gt * S5_GROUP, gt * S5_STATE).astype(BF16)

    def pack_out(w):
        w = w.reshape(S5_TILES, gt, S5_GROUP, S5_STATE)
        return jnp.einsum('kgnp,gh->kgphn', w, eye).reshape(S5_TILES, gt * S5_STATE, gt * S5_GROUP)

    cw = jnp.concatenate([pack_out(c_re.astype(F32)), -pack_out(c_im.astype(F32))], axis=1).astype(BF16)
    return (pack_in(bb_re), pack_in(bb_im), abar_re.reshape(1, S5_LANES), abar_im.reshape(1, S5_LANES), cw)


def _make_gdn_kernel(T, c, hb, cu):
    n = T // c
    span = cu * c
    dh = GDN_HEAD_DIM
    ng = c // SUBLANES
    pad0 = SUBLANES
    taps = GDN_CONV
    lanes = tuple(slice(j * dh, (j + 1) * dh) for j in range(hb))
    blk = min(GDN_SOLVE_BLOCK, c)

    def kernel(q_ref, k_ref, v_ref, z_ref, ab_ref, cq_ref, ck_ref, cv_ref, wq_ref, wk_ref, wv_ref,
               alog_ref, dtb_ref, ng_ref, s0_ref, o_ref, s_ref, pq_ref, pk_ref, pv_ref):
        h0 = pl.program_id(1) * hb
        s_ref[0] = s0_ref[0]
        for pref, cref in ((pq_ref, cq_ref), (pk_ref, ck_ref), (pv_ref, cv_ref)):
            pref[0:pad0, :] = jnp.zeros((pad0, hb * dh), F32)
            pref[pad0 - (taps - 1):pad0, :] = cref[0]

        r = lax.broadcasted_iota(jnp.int32, (c, c), 0)
        col = lax.broadcasted_iota(jnp.int32, (c, c), 1)
        incl = r >= col
        strict = r > col
        eye = r == col
        l_incl = incl.astype(F32)
        lane = lax.broadcasted_iota(jnp.int32, (1, LANES), 1)
        neg_decay_rate = -jnp.exp(alog_ref[...])

        def conv(src_ref, pref, w_ref, t0):
            pref[pad0:pad0 + span, :] = src_ref[0, pl.ds(t0, span), :]
            acc = w_ref[0:1, :] * pref[pl.ds(pad0 - (taps - 1), span), :]
            for i in range(1, taps):
                acc = acc + w_ref[i:i + 1, :] * pref[pl.ds(pad0 - (taps - 1) + i, span), :]
            pref[0:pad0, :] = pref[span:span + pad0, :]
            return _silu(acc)

        def pick(sel, m):
            return jnp.sum(jnp.where(sel, m, 0.0), axis=-1, keepdims=True)

        def l2n(x):
            return x * lax.rsqrt(jnp.sum(x * x, axis=-1, keepdims=True) + EPS)

        def chunk(ci, carry):
            t0 = pl.multiple_of(ci * span, span)
            pairs = [(u, j) for u in range(cu) for j in range(hb)]
            hs = range(len(pairs))
            q_all = conv(q_ref, pq_ref, wq_ref, t0)
            k_all = conv(k_ref, pk_ref, wk_ref, t0)
            v_all = conv(v_ref, pv_ref, wv_ref, t0)
            ab = ab_ref[0, pl.ds(t0, span), :]
            g_all = neg_decay_rate * _softplus(ab + dtb_ref[...])
            gcum_all = [_dot_f32(l_incl, g_all[u * c:(u + 1) * c]) for u in range(cu)]
            beta_all = _sigmoid(ab)

            def rows(u):
                return slice(u * c, (u + 1) * c)

            q = [l2n(q_all[rows(u), lanes[j]]) * (dh ** -0.5) for u, j in pairs]
            k = [l2n(k_all[rows(u), lanes[j]]) for u, j in pairs]
            v = [v_all[rows(u), lanes[j]] for u, j in pairs]
            gcum = [pick(lane == h0 + j, gcum_all[u]) for u, j in pairs]
            beta = [pick(lane == h0 + j + GDN_HEADS, beta_all[rows(u)]) for u, j in pairs]
            gcum_row = [jnp.sum(jnp.where(eye, g, 0.0), axis=0, keepdims=True) for g in gcum]
            decay = [jnp.where(incl, jnp.exp(jnp.where(incl, g - gr, 0.0)), 0.0) for g, gr in zip(gcum, gcum_row)]
            kb = [x.astype(BF16) for x in k]
            kk = [_dot_nt(x, x) for x in kb]
            qk = [_dot_nt(x.astype(BF16), y) for x, y in zip(q, kb)]
            a = [jnp.where(strict, b * m * d, 0.0) for b, m, d in zip(beta, kk, decay)]

            inv = [[jnp.where(eye[g * SUBLANES:(g + 1) * SUBLANES], 1.0, 0.0).astype(F32) for g in range(ng)]
                   for _ in hs]
            a_g = [[m[g * SUBLANES:(g + 1) * SUBLANES] for g in range(ng)] for m in a]
            gpb = blk // SUBLANES
            for step in range(blk - 1):
                for b in range(c // blk):
                    g0, jr = divmod(b * blk + step, SUBLANES)
                    for j in hs:
                        row = inv[j][g0][jr:jr + 1, :]
                        for g in range(g0, (b + 1) * gpb):
                            inv[j][g] = inv[j][g] - a_g[j][g][:, b * blk + step:b * blk + step + 1] * row
            inv = [jnp.concatenate(m, axis=0) for m in inv]
            size = blk
            while size < c:
                sh = size.bit_length() - 1
                off_mask = jnp.logical_and(r >> (sh + 1) == col >> (sh + 1), r >> sh != col >> sh)
                off = [jnp.where(off_mask, m, 0.0).astype(BF16) for m in a]
                inv_b = [m.astype(BF16) for m in inv]
                t = [_dot(off[j], inv_b[j]).astype(BF16) for j in hs]
                inv = [inv[j] - _dot(inv_b[j], t[j]) for j in hs]
                size *= 2

            eg = [jnp.exp(g) for g in gcum]
            inv_b = [m.astype(BF16) for m in inv]
            rhs = [jnp.concatenate([beta[j] * v[j], (beta[j] * eg[j]) * k[j]], axis=1) for j in hs]
            rhs_hi = [m.astype(BF16) for m in rhs]
            rhs_lo = [(m - mh.astype(F32)).astype(BF16) for m, mh in zip(rhs, rhs_hi)]
            sol = [_dot(inv_b[j], rhs_hi[j]) + _dot(inv_b[j], rhs_lo[j]) for j in hs]
            wq = [jnp.concatenate([sol[j][:, dh:2 * dh].astype(BF16), (q[j] * eg[j]).astype(BF16)], axis=0) for j in hs]
            qkd = [(qk[j] * decay[j]).astype(BF16) for j in hs]
            g_last = [gcum[j][c - 1:c, :] for j in hs]
            kd = [(k[j] * jnp.exp(g_last[j] - gcum[j])).astype(BF16) for j in hs]

            for u in range(cu):
                tu = pl.multiple_of(t0 + u * c, c)
                ids = [u * hb + j for j in range(hb)]
                s = [s_ref[0, j] for j in range(hb)]
                sb = [x.astype(BF16) for x in s]
                ws = [_dot(wq[i], sb[j]) for j, i in enumerate(ids)]
                vb = [(sol[i][:, 0:dh] - ws[j][0:c]).astype(BF16) for j, i in enumerate(ids)]
                o = [ws[j][c:2 * c] + _dot(qkd[i], vb[j]) for j, i in enumerate(ids)]
                for j, i in enumerate(ids):
                    s_ref[0, j] = s[j] * jnp.exp(g_last[i]) + lax.dot_general(
                        kd[i], vb[j], (((0,), (0,)), ((), ())), preferred_element_type=F32)
                for j, ls in enumerate(lanes):
                    y = _rms(o[j]) * ng_ref[...] * _silu(z_ref[0, pl.ds(tu, c), ls])
                    o_ref[0, pl.ds(tu, c), ls] = y.astype(BF16)
            return carry

        lax.fori_loop(0, n // cu, chunk, 0)

    return kernel


def _gdn_core(qkv, z, ab, cbuf, conv_w, alog_row, dtb_row, norm_g, s0):
    B, T, _ = qkv.shape
    c = min(GDN_CHUNK, T)
    H, dh = GDN_HEADS, GDN_HEAD_DIM
    hb = GDN_HEADS_PER_STEP
    G = H // hb

    def seq(off):
        return pl.BlockSpec((1, T, hb * dh), lambda b, h: (b, 0, off + h))

    def hist(off):
        return pl.BlockSpec((1, GDN_CONV - 1, hb * dh), lambda b, h: (b, 0, off + h))

    def wconv(off):
        return pl.BlockSpec((GDN_CONV, hb * dh), lambda b, h: (0, off + h))

    st_spec = pl.BlockSpec((1, hb, dh, dh), lambda b, h: (b, h, 0, 0))
    cu = GDN_CHUNKS_PER_STEP if (T // c) % GDN_CHUNKS_PER_STEP == 0 else 1
    pad = pltpu.VMEM((cu * c + SUBLANES, hb * dh), F32)
    return pl.pallas_call(
        _make_gdn_kernel(T, c, hb, cu),
        grid=(B, G),
        in_specs=[seq(0), seq(G), seq(2 * G), seq(0),
                  pl.BlockSpec((1, T, LANES), lambda b, h: (b, 0, 0)),
                  hist(0), hist(G), hist(2 * G), wconv(0), wconv(G), wconv(2 * G),
                  _resident((1, LANES)), _resident((1, LANES)), _resident((1, dh)), st_spec],
        out_specs=[seq(0), st_spec],
        out_shape=[jax.ShapeDtypeStruct((B, T, D_MODEL), BF16), jax.ShapeDtypeStruct((B, H, dh, dh), F32)],
        scratch_shapes=[pad, pad, pad],
        compiler_params=_cparams(("parallel", "parallel")),
        name="gdn_core",
    )(qkv, qkv, qkv, z, ab, cbuf, cbuf, cbuf, conv_w, conv_w, conv_w, alog_row, dtb_row, norm_g, s0)


def _make_conv_kernel(T, tq, rsub, lchunk):
    halo = CNV_HALO
    first_hist = halo - (CNV_WIDTH - 1)

    shifted_rows = tq + halo - SUBLANES

    def kernel(u_ref, cb_ref, w_ref, b_ref, lg_ref, lb_ref, y_ref, us_ref, acc_ref):
        t = pl.program_id(1)
        t0 = pl.multiple_of(t * tq, tq)

        @pl.when(t == 0)
        def _():
            us_ref[0, 0:halo, :] = jnp.zeros((halo, D_MODEL), F32)
            us_ref[0, first_hist:halo, :] = cb_ref[0]

        @pl.when(t > 0)
        def _():
            us_ref[0, 0:halo, :] = u_ref[0, pl.ds(t0 - halo, halo), :]

        us_ref[0, halo:halo + tq, :] = u_ref[0, pl.ds(t0, tq), :]
        for b in range(1, SUBLANES):
            us_ref[b, 0:shifted_rows, :] = us_ref[0, b:b + shifted_rows, :]

        for lc in range(D_MODEL // lchunk):
            ls = slice(lc * lchunk, (lc + 1) * lchunk)

            def rows(ri, carry):
                r0 = pl.multiple_of(ri * rsub, rsub)
                acc = b_ref[:, ls]
                for i in range(CNV_WIDTH):
                    a, b = divmod(first_hist + i, SUBLANES)
                    acc = acc + w_ref[i:i + 1, ls] * us_ref[b, pl.ds(r0 + a * SUBLANES, rsub), ls]
                acc_ref[pl.ds(r0, rsub), ls] = acc
                return carry

            lax.fori_loop(0, tq // rsub, rows, 0)

        y = acc_ref[...]
        yc = y - jnp.mean(y, axis=-1, keepdims=True)
        var = jnp.mean(yc * yc, axis=-1, keepdims=True)
        y_ref[0] = _silu(yc * lax.rsqrt(var + EPS) * lg_ref[...] + lb_ref[...]).astype(BF16)

    return kernel


def _conv_core(u, cbuf, dw_w, dw_b, ln_g, ln_b):
    B, T, _ = u.shape
    tq = min(256, T)
    rsub = min(32, tq)
    row = _resident((1, D_MODEL))
    return pl.pallas_call(
        _make_conv_kernel(T, tq, rsub, 512),
        grid=(B, T // tq),
        in_specs=[pl.BlockSpec((1, T, D_MODEL), lambda b, t: (b, 0, 0)),
                  pl.BlockSpec((1, CNV_WIDTH - 1, D_MODEL), lambda b, t: (b, 0, 0)),
                  _resident((CNV_WIDTH, D_MODEL)), row, row, row],
        out_specs=pl.BlockSpec((1, tq, D_MODEL), lambda b, t: (b, t, 0)),
        out_shape=jax.ShapeDtypeStruct((B, T, D_MODEL), BF16),
        scratch_shapes=[pltpu.VMEM((SUBLANES, tq + CNV_HALO, D_MODEL), F32), pltpu.VMEM((tq, D_MODEL), F32)],
        compiler_params=_cparams(("parallel", "arbitrary")),
        name="cnv_core",
    )(u, cbuf, dw_w, dw_b, ln_g, ln_b)


def _row(v):
    return v.astype(F32).reshape(1, -1)


def _pad_lanes(v):
    return jnp.pad(v.astype(F32), (0, LANES - v.shape[0])).reshape(1, LANES)


def _prep_weights(p):
    nff = D_FF // FF_CHUNK
    w = {}
    w['w1'] = [p['ffn_w1'][i].astype(BF16).reshape(D_MODEL, nff, FF_CHUNK).transpose(1, 0, 2) for i in range(DEPTH)]
    w['w2'] = [p['ffn_w2'][i].astype(BF16).reshape(nff, FF_CHUNK, D_MODEL) for i in range(DEPTH)]
    w['sb_qkv'] = p['sb_w_qkv'][0].astype(BF16)
    w['sb_o'] = p['sb_w_o'][0].astype(BF16)
    w['s5'] = _s5_params(p['s5_a_re'][0], p['s5_a_im'][0], p['s5_log_dt'][0], p['s5_b_re'][0], p['s5_b_im'][0],
                         p['s5_c_re'][0], p['s5_c_im'][0])
    w['s5_glu'] = p['s5_w_glu'][0].astype(BF16)
    w_in = p['gdn_w_in'][0]
    w['gdn_in'] = w_in[:, :4 * D_MODEL].astype(BF16)
    w['gdn_ab'] = jnp.pad(w_in[:, 4 * D_MODEL:], ((0, 0), (0, LANES - 2 * GDN_HEADS))).astype(BF16)
    w['gdn_o'] = p['gdn_w_o'][0].astype(BF16)
    w['pw1'] = p['cnv_w_pw1'][0].astype(BF16)
    w['pw2'] = p['cnv_w_pw2'][0].astype(BF16)
    return w


def _trunk(x, mod, sb_k, sb_v, s5_re, s5_im, gdn_s, gdn_cb, cnv_cb, p, w):
    B, T, _ = x.shape
    out = {}
    g_out = _row(p['norm_out_g'])

    def tail(i, x, y, w_pre, glu=False):
        return _tail(x, y, mod[i], w_pre, _row(p['norm_ffn_g'][i]), w['w1'][i], w['w2'][i], g_out,
                     glu=glu, final=(i == DEPTH - 1))

    q, k, v, kb, vb = _qkv_proj(x, mod[0], _row(p['norm_mix_g'][0]), w['sb_qkv'])
    if sb_k is None:
        o = _sb_attention(q, kb, vb)
    else:
        P = sb_k.shape[1]
        o = _sb_attention(q, kb, vb, sb_k.reshape(B, P, D_MODEL).astype(BF16),
                          sb_v.reshape(B, P, D_MODEL).astype(BF16))
    out['sb_k'] = k.reshape(1, B, T, SB_HEADS, SB_HEAD_DIM)
    out['sb_v'] = v.reshape(1, B, T, SB_HEADS, SB_HEAD_DIM)
    x = tail(0, x, o, w['sb_o'])

    wre, wim, a_re, a_im, cw = w['s5']
    yt, s_re, s_im = _s5_scan(x, mod[1][:, 0, :], mod[1][:, 1, :], _row(p['norm_mix_g'][1]),
                              wre, wim, a_re, a_im, cw, _row(p['s5_d'][0]),
                              s5_re.reshape(B, S5_LANES), s5_im.reshape(B, S5_LANES))
    out['s5_re'] = s_re.reshape(1, B, S5_GROUPS, S5_STATE)
    out['s5_im'] = s_im.reshape(1, B, S5_GROUPS, S5_STATE)
    x = tail(1, x, yt, w['s5_glu'], glu=True)

    qkv, z, ab = _gdn_in_proj(x, mod[2], _row(p['norm_mix_g'][2]), w['gdn_in'], w['gdn_ab'])
    o, s_new = _gdn_core(qkv, z, ab, gdn_cb, p['gdn_conv_w'][0].astype(F32), _pad_lanes(p['gdn_a_log'][0]),
                         _pad_lanes(p['gdn_dt_bias'][0]), _row(p['gdn_norm_g'][0]), gdn_s)
    out['gdn'] = s_new[None]
    hist = jnp.concatenate([gdn_cb, qkv], axis=1) if T < GDN_CONV - 1 else qkv
    out['gdn_conv'] = hist[None, :, hist.shape[1] - (GDN_CONV - 1):]
    x = tail(2, x, o, w['gdn_o'])

    u = _pw1_proj(x, mod[3], _row(p['norm_mix_g'][3]), w['pw1'])
    ya = _conv_core(u, cnv_cb, p['cnv_dw_w'][0].astype(F32), _row(p['cnv_dw_b'][0]), _row(p['cnv_ln_g'][0]),
                    _row(p['cnv_ln_b'][0]))
    hist = jnp.concatenate([cnv_cb, u], axis=1) if T < CNV_WIDTH - 1 else u
    out['conv'] = hist[None, :, hist.shape[1] - (CNV_WIDTH - 1):]
    x = tail(3, x, ya, w['pw2'])
    return x, out


def kernel(x_prompt, x_sample, c_prompt, c_sample, cache_sb_k, cache_sb_v, state_s5_re, state_s5_im, state_gdn, state_gdn_conv, state_conv, ada_w, ada_b, norm_mix_g, norm_ffn_g, norm_out_g, ffn_w1, ffn_w2, sb_w_qkv, sb_w_o, s5_a_re, s5_a_im, s5_log_dt, s5_b_re, s5_b_im, s5_c_re, s5_c_im, s5_d, s5_w_glu, gdn_w_in, gdn_conv_w, gdn_a_log, gdn_dt_bias, gdn_norm_g, gdn_w_o, cnv_w_pw1, cnv_dw_w, cnv_dw_b, cnv_ln_g, cnv_ln_b, cnv_w_pw2):
    p = dict(norm_mix_g=norm_mix_g, norm_ffn_g=norm_ffn_g, norm_out_g=norm_out_g, ffn_w1=ffn_w1, ffn_w2=ffn_w2,
             sb_w_qkv=sb_w_qkv, sb_w_o=sb_w_o, s5_a_re=s5_a_re, s5_a_im=s5_a_im, s5_log_dt=s5_log_dt,
             s5_b_re=s5_b_re, s5_b_im=s5_b_im, s5_c_re=s5_c_re, s5_c_im=s5_c_im, s5_d=s5_d, s5_w_glu=s5_w_glu,
             gdn_w_in=gdn_w_in, gdn_conv_w=gdn_conv_w, gdn_a_log=gdn_a_log, gdn_dt_bias=gdn_dt_bias,
             gdn_norm_g=gdn_norm_g, gdn_w_o=gdn_w_o, cnv_w_pw1=cnv_w_pw1, cnv_dw_w=cnv_dw_w, cnv_dw_b=cnv_dw_b,
             cnv_ln_g=cnv_ln_g, cnv_ln_b=cnv_ln_b, cnv_w_pw2=cnv_w_pw2)
    w = _prep_weights(p)
    bp, bs = x_prompt.shape[0], x_sample.shape[0]
    mod = _ada(jnp.concatenate([c_prompt, c_sample], axis=0), ada_w, ada_b)
    mod = mod.reshape(DEPTH, bp + bs, 6, D_MODEL)
    mod_p = [mod[i, :bp] for i in range(DEPTH)]
    mod_s = [mod[i, bp:] for i in range(DEPTH)]

    yp, op = _trunk(
        x_prompt, mod_p, None, None,
        jnp.zeros((bp, S5_GROUPS, S5_STATE), F32), jnp.zeros((bp, S5_GROUPS, S5_STATE), F32),
        jnp.zeros((bp, GDN_HEADS, GDN_HEAD_DIM, GDN_HEAD_DIM), F32),
        jnp.zeros((bp, GDN_CONV - 1, 3 * D_MODEL), F32), jnp.zeros((bp, CNV_WIDTH - 1, D_MODEL), F32), p, w)
    ys, os_ = _trunk(
        x_sample, mod_s, cache_sb_k[0], cache_sb_v[0], state_s5_re[0], state_s5_im[0],
        state_gdn[0], state_gdn_conv[0], state_conv[0], p, w)
    return (yp, ys, op['sb_k'], op['sb_v'], os_['sb_k'], os_['sb_v'],
            op['s5_re'], op['s5_im'], os_['s5_re'], os_['s5_im'],
            op['gdn'], op['gdn_conv'], os_['gdn'], os_['gdn_conv'],
            op['conv'], os_['conv'])
```

```python
import functools

import jax
import jax.numpy as jnp
from jax import lax
from jax.experimental import pallas as pl
from jax.experimental.pallas import tpu as pltpu

F32 = jnp.float32
BF16 = jnp.bfloat16

D_MODEL = 1024
DEPTH = 4
EPS = 1e-6
D_FF = 4 * D_MODEL

SB_HEADS = 16
SB_HEAD_DIM = 64
SB_Q_ROWS = 512
SB_KEY_BLOCK = 256
SB_SKIP = 104.0

S5_GROUPS = 64
S5_GROUP = 16
S5_STATE = 64
S5_LANES = S5_GROUPS * S5_STATE
S5_GROUPS_PER_TILE = 16
S5_TILES = S5_GROUPS // S5_GROUPS_PER_TILE

GDN_HEADS = 8
GDN_HEAD_DIM = 128
GDN_CONV = 4
GDN_CHUNK = 64
GDN_HEADS_PER_STEP = 4
GDN_CHUNKS_PER_STEP = 4
GDN_SOLVE_BLOCK = 16

CNV_WIDTH = 31
CNV_HALO = 32

LANES = 128
SUBLANES = 8
ROW_TILE = 512
FF_CHUNK = 512
FF_UNROLL = 4
V7X_VMEM_BYTES = 64 * 2 ** 20
VMEM_LIMIT = 56 * 2 ** 20


def _cparams(sem):
    return pltpu.CompilerParams(dimension_semantics=sem, vmem_limit_bytes=VMEM_LIMIT)


def _resident(shape):
    nd = len(shape)
    return pl.BlockSpec(shape, lambda *_: (0,) * nd, pipeline_mode=pl.Buffered(1))


def _dot(a, b):
    return jnp.dot(a, b, preferred_element_type=F32)


def _dot_nt(a, b):
    return lax.dot_general(a, b, (((1,), (1,)), ((), ())), preferred_element_type=F32)


def _dot_f32(a, b):
    return jnp.dot(a, b, preferred_element_type=F32, precision=lax.Precision.HIGHEST)


def _sigmoid(x):
    return 1.0 / (1.0 + jnp.exp(-x))


def _silu(x):
    return x * _sigmoid(x)


def _softplus(x):
    return jnp.maximum(x, 0.0) + jnp.log(1.0 + jnp.exp(-jnp.abs(x)))


def _rms(x):
    return x * lax.rsqrt(jnp.mean(x * x, axis=-1, keepdims=True) + EPS)


def _norm_mod(x, g, sc, sh):
    return _rms(x) * g * (1.0 + sc) + sh


def _ada_kernel(c_ref, w_ref, b_ref, o_ref):
    cs = _silu(c_ref[...]).astype(BF16)
    o_ref[0] = _dot(cs, w_ref[0].astype(BF16)) + b_ref[0]


def _ada(c_all, ada_w, ada_b):
    nb = c_all.shape[0]
    ncol = ada_w.shape[-1] // D_MODEL
    return pl.pallas_call(
        _ada_kernel,
        grid=(DEPTH, ncol),
        in_specs=[
            pl.BlockSpec((nb, D_MODEL), lambda l, j: (0, 0)),
            pl.BlockSpec((1, D_MODEL, D_MODEL), lambda l, j: (l, 0, j)),
            pl.BlockSpec((1, 1, D_MODEL), lambda l, j: (l, 0, j)),
        ],
        out_specs=pl.BlockSpec((1, nb, D_MODEL), lambda l, j: (l, 0, j)),
        out_shape=jax.ShapeDtypeStruct((DEPTH, nb, ncol * D_MODEL), F32),
        compiler_params=_cparams(("arbitrary", "arbitrary")),
        name="ada_mod",
    )(c_all, ada_w, ada_b.reshape(DEPTH, 1, -1))


def _row_specs(tm):
    x_spec = pl.BlockSpec((1, tm, D_MODEL), lambda b, t: (b, t, 0))
    mod_spec = pl.BlockSpec((1, 6, D_MODEL), lambda b, t: (b, 0, 0))
    return x_spec, mod_spec


def _mixer_in(x_ref, mod_ref, g_ref):
    return _norm_mod(x_ref[0], g_ref[...], mod_ref[0, 1:2, :], mod_ref[0, 0:1, :]).astype(BF16)


def _qkv_kernel(x_ref, mod_ref, g_ref, w_ref, q_ref, k_ref, v_ref, kb_ref, vb_ref):
    hm = _mixer_in(x_ref, mod_ref, g_ref)
    q_ref[0] = (_dot(hm, w_ref[:, 0:D_MODEL]) * (SB_HEAD_DIM ** -0.5)).astype(BF16)
    k = _dot(hm, w_ref[:, D_MODEL:2 * D_MODEL])
    k_ref[0] = k
    kb_ref[0] = k.astype(BF16)
    v = _dot(hm, w_ref[:, 2 * D_MODEL:3 * D_MODEL])
    v_ref[0] = v
    vb_ref[0] = v.astype(BF16)


def _qkv_proj(x, mod, g, w):
    B, T, _ = x.shape
    tm = min(ROW_TILE, T)
    x_spec, mod_spec = _row_specs(tm)
    out = lambda dt: jax.ShapeDtypeStruct((B, T, D_MODEL), dt)
    return pl.pallas_call(
        _qkv_kernel,
        grid=(B, T // tm),
        in_specs=[x_spec, mod_spec, _resident((1, D_MODEL)), _resident((D_MODEL, 3 * D_MODEL))],
        out_specs=[x_spec] * 5,
        out_shape=[out(BF16), out(F32), out(F32), out(BF16), out(BF16)],
        compiler_params=_cparams(("parallel", "parallel")),
        name="sb_qkv",
    )(x, mod, g, w)


def _gdn_in_kernel(x_ref, mod_ref, g_ref, w_ref, wab_ref, qkv_ref, z_ref, ab_ref):
    hm = _mixer_in(x_ref, mod_ref, g_ref)
    for j in range(3):
        cols = slice(j * D_MODEL, (j + 1) * D_MODEL)
        qkv_ref[0, :, cols] = _dot(hm, w_ref[:, cols])
    z_ref[0] = _dot(hm, w_ref[:, 3 * D_MODEL:4 * D_MODEL])
    ab_ref[0] = _dot(hm, wab_ref[...])


def _gdn_in_proj(x, mod, g, w, wab):
    B, T, _ = x.shape
    tm = min(ROW_TILE, T)
    x_spec, mod_spec = _row_specs(tm)
    return pl.pallas_call(
        _gdn_in_kernel,
        grid=(B, T // tm),
        in_specs=[x_spec, mod_spec, _resident((1, D_MODEL)), _resident((D_MODEL, 4 * D_MODEL)),
                  _resident((D_MODEL, LANES))],
        out_specs=[pl.BlockSpec((1, tm, 3 * D_MODEL), lambda b, t: (b, t, 0)), x_spec,
                   pl.BlockSpec((1, tm, LANES), lambda b, t: (b, t, 0))],
        out_shape=[jax.ShapeDtypeStruct((B, T, 3 * D_MODEL), F32), jax.ShapeDtypeStruct((B, T, D_MODEL), F32),
                   jax.ShapeDtypeStruct((B, T, LANES), F32)],
        compiler_params=_cparams(("parallel", "parallel")),
        name="gdn_in",
    )(x, mod, g, w, wab)


def _pw1_kernel(x_ref, mod_ref, g_ref, w_ref, u_ref):
    hm = _mixer_in(x_ref, mod_ref, g_ref)
    u_ref[0] = _dot(hm, w_ref[:, 0:D_MODEL]) * _sigmoid(_dot(hm, w_ref[:, D_MODEL:2 * D_MODEL]))


def _pw1_proj(x, mod, g, w):
    B, T, _ = x.shape
    tm = min(ROW_TILE, T)
    x_spec, mod_spec = _row_specs(tm)
    return pl.pallas_call(
        _pw1_kernel,
        grid=(B, T // tm),
        in_specs=[x_spec, mod_spec, _resident((1, D_MODEL)), _resident((D_MODEL, 2 * D_MODEL))],
        out_specs=x_spec,
        out_shape=jax.ShapeDtypeStruct((B, T, D_MODEL), F32),
        compiler_params=_cparams(("parallel", "parallel")),
        name="cnv_pw1",
    )(x, mod, g, w)


def _tail_kernel(x_ref, y_ref, mod_ref, wpre_ref, g_ref, w1_ref, w2_ref, gout_ref, o_ref, acc_ref, *, glu, final):
    pre = _dot(y_ref[0], wpre_ref[...])
    if glu:
        pre = pre[:, 0:D_MODEL] * _sigmoid(pre[:, D_MODEL:2 * D_MODEL])
    x1 = x_ref[0] + mod_ref[0, 2:3, :] * pre
    o_ref[0] = x1
    hf = _norm_mod(x1, g_ref[...], mod_ref[0, 4:5, :], mod_ref[0, 3:4, :]).astype(BF16)

    def ff_pair(k):
        a = [jnp.maximum(_dot(hf, w1_ref[k + u]), 0.0) for u in range(FF_UNROLL)]
        p = [_dot((a[u] * a[u]).astype(BF16), w2_ref[k + u]) for u in range(FF_UNROLL)]
        return functools.reduce(lambda s, t: s + t, p)

    acc_ref[...] = ff_pair(0)

    def ff_step(k, carry):
        acc_ref[...] += ff_pair(k * FF_UNROLL)
        return carry

    lax.fori_loop(1, D_FF // (FF_CHUNK * FF_UNROLL), ff_step, 0)
    x2 = o_ref[0] + mod_ref[0, 5:6, :] * acc_ref[...]
    if final:
        x2 = _rms(x2) * gout_ref[...]
    o_ref[0] = x2


def _tail(x, y, mod, w_pre, g_ffn, w1c, w2c, g_out, *, glu, final):
    B, T, _ = x.shape
    tm = min(ROW_TILE, T)
    x_spec, mod_spec = _row_specs(tm)
    ky = y.shape[-1]
    nff = D_FF // FF_CHUNK
    return pl.pallas_call(
        functools.partial(_tail_kernel, glu=glu, final=final),
        grid=(B, T // tm),
        in_specs=[x_spec, pl.BlockSpec((1, tm, ky), lambda b, t: (b, t, 0)), mod_spec,
                  _resident(w_pre.shape), _resident((1, D_MODEL)),
                  _resident((nff, D_MODEL, FF_CHUNK)), _resident((nff, FF_CHUNK, D_MODEL)),
                  _resident((1, D_MODEL))],
        out_specs=x_spec,
        out_shape=jax.ShapeDtypeStruct((B, T, D_MODEL), F32),
        scratch_shapes=[pltpu.VMEM((tm, D_MODEL), F32)],
        compiler_params=_cparams(("parallel", "parallel")),
        name="tail",
    )(x, y, mod, w_pre, g_ffn, w1c, w2c, g_out)


def _make_attn_kernel(tq, past, bq, bkn, bkp):
    nq = tq // bq
    ndiag = bq // bkn
    npast = past // bkp
    hd = SB_HEAD_DIM
    heads = tuple(slice(h * hd, (h + 1) * hd) for h in range(LANES // hd))

    def suffix_matrix(n):
        r = lax.broadcasted_iota(jnp.int32, (n, n), 0)
        c = lax.broadcasted_iota(jnp.int32, (n, n), 1)
        m = (r >= c).astype(BF16)
        return jnp.concatenate([m, m], axis=0)

    def kernel(*refs):
        if past:
            q_ref, k_ref, v_ref, kp_ref, vp_ref, o_ref = refs
        else:
            q_ref, k_ref, v_ref, o_ref = refs
        m_new = suffix_matrix(bkn)
        m_past = suffix_matrix(bkp) if past else None

        def scores(qs, ks, masks, m):
            zs = [_dot_nt(q, k) for q, k in zip(qs, ks)]
            sps = [_softplus(z) for z in zs]
            sps = [sp if mk is None else jnp.where(mk, sp, 0.0) for sp, mk in zip(sps, masks)]
            his = [sp.astype(BF16) for sp in sps]
            los = [(sp - hi.astype(F32)).astype(BF16) for sp, hi in zip(sps, his)]
            return zs, [_dot(jnp.concatenate([hi, lo], axis=1), m) for hi, lo in zip(his, los)]

        def weighted(zs, sufs, carries, vs, masks):
            ws = [jnp.exp(z - suf - c) for z, suf, c in zip(zs, sufs, carries)]
            ws = [w if mk is None else jnp.where(mk, w, 0.0) for w, mk in zip(ws, masks)]
            return [_dot(w.astype(BF16), v) for w, v in zip(ws, vs)]

        def live(carries):
            lowest = carries[0]
            for c in carries[1:]:
                lowest = jnp.minimum(lowest, c)
            return jnp.min(lowest) < SB_SKIP

        def q_block(i, has_older):
            r0 = pl.multiple_of(i * bq, bq) if has_older else 0
            qs = [q_ref[0, pl.ds(r0, bq), ls] for ls in heads]
            carries = [jnp.zeros((bq, 1), F32) for _ in heads]
            accs = [jnp.zeros((bq, hd), F32) for _ in heads]

            blocks = [(d * bkn, d * bkn, True) for d in reversed(range(ndiag))]
            if has_older:
                blocks.append((0, -bkn, False))
            items = [(rs, off, masked, h) for rs, off, masked in blocks for h in range(len(heads))]

            def inside(ref):
                def start(off):
                    return pl.multiple_of(r0 + off, bkn) if has_older else off
                return [ref[0, pl.ds(start(off), bkn), heads[h]] for _, off, _, h in items]

            visible = [(lax.broadcasted_iota(jnp.int32, (bq - rs, bkn), 1)
                        < lax.broadcasted_iota(jnp.int32, (bq - rs, bkn), 0)) if masked else None
                       for rs, _, masked, _ in items]
            zs, sufs = scores([qs[h][rs:] for rs, _, _, h in items], inside(k_ref), visible, m_new)
            newer = []
            for (rs, _, _, h), suf in zip(items, sufs):
                newer.append(carries[h][rs:])
                mass = suf[:, 0:1]
                if rs:
                    mass = jnp.concatenate([jnp.zeros((rs, 1), F32), mass], axis=0)
                carries[h] = carries[h] + mass
            for (rs, _, _, h), part in zip(items, weighted(zs, sufs, newer, inside(v_ref), visible)):
                if rs:
                    part = jnp.concatenate([jnp.zeros((rs, hd), F32), part], axis=0)
                accs[h] = accs[h] + part

            def sweep(first, n_blocks, kref, vref, bk, m, carries, accs):
                def cond(st):
                    return jnp.logical_and(st[0] < n_blocks, live(st[1]))

                def body(st):
                    jj, cs, as_ = st
                    k0 = pl.multiple_of((n_blocks - 1 - jj) * bk, bk)
                    unmasked = [None] * len(heads)
                    zs, sufs = scores(qs, [kref[0, pl.ds(k0, bk), ls] for ls in heads], unmasked, m)
                    parts = weighted(zs, sufs, cs, [vref[0, pl.ds(k0, bk), ls] for ls in heads], unmasked)
                    return (jj + 1, tuple(c + suf[:, 0:1] for c, suf in zip(cs, sufs)),
                            tuple(a + p for a, p in zip(as_, parts)))

                _, cs, as_ = lax.while_loop(cond, body, (jnp.int32(first), tuple(carries), tuple(accs)))
                return list(cs), list(as_)

            if has_older:
                carries, accs = sweep(1, i * ndiag, k_ref, v_ref, bkn, m_new, carries, accs)
            if past:
                carries, accs = sweep(0, npast, kp_ref, vp_ref, bkp, m_past, carries, accs)
            for h, ls in enumerate(heads):
                o_ref[0, pl.ds(r0, bq), ls] = accs[h].astype(BF16)
            return 0

        q_block(0, False)
        if nq > 1:
            lax.fori_loop(1, nq, lambda i, c: q_block(i, True), 0)

    return kernel


def _sb_attention(q, k, v, k_past=None, v_past=None):
    B, T, _ = q.shape
    past = 0 if k_past is None else k_past.shape[1]
    bq = min(SB_Q_ROWS, T)
    bkn = min(SB_KEY_BLOCK, T)
    seq_spec = pl.BlockSpec((1, T, LANES), lambda b, h: (b, 0, h))
    in_specs = [seq_spec] * 3
    args = [q, k, v]
    if past:
        in_specs += [pl.BlockSpec((1, past, LANES), lambda b, h: (b, 0, h))] * 2
        args += [k_past, v_past]
    return pl.pallas_call(
        _make_attn_kernel(T, past, bq, bkn, SB_KEY_BLOCK),
        grid=(B, D_MODEL // LANES),
        in_specs=in_specs,
        out_specs=seq_spec,
        out_shape=jax.ShapeDtypeStruct((B, T, D_MODEL), BF16),
        compiler_params=_cparams(("parallel", "parallel")),
        name="sb_attn",
    )(*args)


def _gelu_tanh(x):
    return 0.5 * x * (1.0 + jnp.tanh(0.7978845608028654 * (x + 0.044715 * (x * x * x))))


def _make_s5_kernel(c, bb, lane_chunk):
    rows = c * bb
    tile_lanes = S5_GROUPS_PER_TILE * S5_STATE
    tile_cols = S5_GROUPS_PER_TILE * S5_GROUP

    def kernel(x_ref, sh_ref, sc_ref, g_ref, wre_ref, wim_ref, are_ref, aim_ref, cw_ref, d_ref,
               s0r_ref, s0i_ref, y_ref, sr_ref, si_ref, hm_ref, xr_ref, xi_ref):
        @pl.when(pl.program_id(1) == 0)
        def _():
            sr_ref[...] = s0r_ref[...]
            si_ref[...] = s0i_ref[...]

        xt = jnp.swapaxes(x_ref[...], 0, 1)
        hm = _norm_mod(xt, g_ref[...], sc_ref[...][None], sh_ref[...][None])
        hm_ref[...] = hm.reshape(rows, D_MODEL)
        for kt in range(S5_TILES):
            hb = hm_ref[:, kt * tile_cols:(kt + 1) * tile_cols].astype(BF16)
            xr_ref[:, kt * tile_lanes:(kt + 1) * tile_lanes] = _dot(hb, wre_ref[kt])
            xi_ref[:, kt * tile_lanes:(kt + 1) * tile_lanes] = _dot(hb, wim_ref[kt])

        for nt in range(S5_TILES):
            for lc in range(nt * tile_lanes // lane_chunk, (nt + 1) * tile_lanes // lane_chunk):
                ls = slice(lc * lane_chunk, (lc + 1) * lane_chunk)
                a_r = are_ref[:, ls]
                a_i = aim_ref[:, ls]
                s_r = sr_ref[:, ls]
                s_i = si_ref[:, ls]
                for t in range(c):
                    rs = slice(t * bb, (t + 1) * bb)
                    s_r, s_i = (a_r * s_r - a_i * s_i + xr_ref[rs, ls], a_r * s_i + a_i * s_r + xi_ref[rs, ls])
                    xr_ref[rs, ls] = s_r
                    xi_ref[rs, ls] = s_i
                sr_ref[:, ls] = s_r
                si_ref[:, ls] = s_i

            ls = slice(nt * tile_lanes, (nt + 1) * tile_lanes)
            cs = slice(nt * tile_cols, (nt + 1) * tile_cols)
            st = jnp.concatenate([xr_ref[:, ls].astype(BF16), xi_ref[:, ls].astype(BF16)], axis=1)
            yv = _dot(st, cw_ref[nt]) + d_ref[:, cs] * hm_ref[:, cs]
            y_ref[:, :, cs] = jnp.swapaxes(_gelu_tanh(yv).reshape(c, bb, tile_cols), 0, 1).astype(BF16)

    return kernel


def _s5_scan(x, sh, sc, g, wre, wim, a_re, a_im, cw, d, s0r, s0i):
    B, T, _ = x.shape
    bb = min(B, 16)
    c = min(32, T)
    x_spec = pl.BlockSpec((bb, c, D_MODEL), lambda b, t: (b, t, 0))
    b_spec = pl.BlockSpec((bb, D_MODEL), lambda b, t: (b, 0))
    s_spec = pl.BlockSpec((bb, S5_LANES), lambda b, t: (b, 0))
    tile_lanes = S5_GROUPS_PER_TILE * S5_STATE
    tile_cols = S5_GROUPS_PER_TILE * S5_GROUP
    return pl.pallas_call(
        _make_s5_kernel(c, bb, 512),
        grid=(B // bb, T // c),
        in_specs=[x_spec, b_spec, b_spec, _resident((1, D_MODEL)),
                  _resident((S5_TILES, tile_cols, tile_lanes)), _resident((S5_TILES, tile_cols, tile_lanes)),
                  _resident((1, S5_LANES)), _resident((1, S5_LANES)),
                  _resident((S5_TILES, 2 * tile_lanes, tile_cols)), _resident((1, D_MODEL)),
                  s_spec, s_spec],
        out_specs=[x_spec, s_spec, s_spec],
        out_shape=[jax.ShapeDtypeStruct((B, T, D_MODEL), BF16), jax.ShapeDtypeStruct((B, S5_LANES), F32),
                   jax.ShapeDtypeStruct((B, S5_LANES), F32)],
        scratch_shapes=[pltpu.VMEM((c * bb, D_MODEL), F32), pltpu.VMEM((c * bb, S5_LANES), F32),
                        pltpu.VMEM((c * bb, S5_LANES), F32)],
        compiler_params=_cparams(("parallel", "arbitrary")),
        name="s5_scan",
    )(x, sh, sc, g, wre, wim, a_re, a_im, cw, d, s0r, s0i)


def _s5_params(a_re, a_im, log_dt, b_re, b_im, c_re, c_im):
    dt = jnp.exp(log_dt.astype(F32))[:, None]
    ar, ai = a_re.astype(F32), a_im.astype(F32)
    mag = jnp.exp(ar * dt)
    abar_re, abar_im = mag * jnp.cos(ai * dt), mag * jnp.sin(ai * dt)
    den = ar * ar + ai * ai
    f_re = ((abar_re - 1.0) * ar + abar_im * ai) / den
    f_im = (abar_im * ar - (abar_re - 1.0) * ai) / den
    br, bi = b_re.astype(F32), b_im.astype(F32)
    bb_re = f_re[..., None] * br - f_im[..., None] * bi
    bb_im = f_re[..., None] * bi + f_im[..., None] * br
    gt = S5_GROUPS_PER_TILE
    eye = jnp.eye(gt, dtype=F32)

    def pack_in(w):
        w = w.reshape(S5_TILES, gt, S5_STATE, S5_GROUP)
        return jnp.einsum('kgpn,gh->kgnhp', w, eye).reshape(S5_TILES, gt * S5_GROUP, gt * S5_STATE).astype(BF16)

    def pack_out(w):
        w = w.reshape(S5_TILES, gt, S5_GROUP, S5_STATE)
        return jnp.einsum('kgnp,gh->kgphn', w, eye).reshape(S5_TILES, gt * S5_STATE, gt * S5_GROUP)

    cw = jnp.concatenate([pack_out(c_re.astype(F32)), -pack_out(c_im.astype(F32))], axis=1).astype(BF16)
    return (pack_in(bb_re), pack_in(bb_im), abar_re.reshape(1, S5_LANES), abar_im.reshape(1, S5_LANES), cw)


def _make_gdn_kernel(T, c, hb, cu):
    n = T // c
    span = cu * c
    dh = GDN_HEAD_DIM
    ng = c // SUBLANES
    pad0 = SUBLANES
    taps = GDN_CONV
    lanes = tuple(slice(j * dh, (j + 1) * dh) for j in range(hb))
    blk = min(GDN_SOLVE_BLOCK, c)

    def kernel(q_ref, k_ref, v_ref, z_ref, ab_ref, cq_ref, ck_ref, cv_ref, wq_ref, wk_ref, wv_ref,
               alog_ref, dtb_ref, ng_ref, s0_ref, o_ref, s_ref, pq_ref, pk_ref, pv_ref):
        h0 = pl.program_id(1) * hb
        s_ref[0] = s0_ref[0]
        for pref, cref in ((pq_ref, cq_ref), (pk_ref, ck_ref), (pv_ref, cv_ref)):
            pref[0:pad0, :] = jnp.zeros((pad0, hb * dh), F32)
            pref[pad0 - (taps - 1):pad0, :] = cref[0]

        r = lax.broadcasted_iota(jnp.int32, (c, c), 0)
        col = lax.broadcasted_iota(jnp.int32, (c, c), 1)
        incl = r >= col
        strict = r > col
        eye = r == col
        l_incl = incl.astype(F32)
        lane = lax.broadcasted_iota(jnp.int32, (1, LANES), 1)
        neg_decay_rate = -jnp.exp(alog_ref[...])

        def conv(src_ref, pref, w_ref, t0):
            pref[pad0:pad0 + span, :] = src_ref[0, pl.ds(t0, span), :]
            acc = w_ref[0:1, :] * pref[pl.ds(pad0 - (taps - 1), span), :]
            for i in range(1, taps):
                acc = acc + w_ref[i:i + 1, :] * pref[pl.ds(pad0 - (taps - 1) + i, span), :]
            pref[0:pad0, :] = pref[span:span + pad0, :]
            return _silu(acc)

        def pick(sel, m):
            return jnp.sum(jnp.where(sel, m, 0.0), axis=-1, keepdims=True)

        def l2n(x):
            return x * lax.rsqrt(jnp.sum(x * x, axis=-1, keepdims=True) + EPS)

        def chunk(ci, carry):
            t0 = pl.multiple_of(ci * span, span)
            pairs = [(u, j) for u in range(cu) for j in range(hb)]
            hs = range(len(pairs))
            q_all = conv(q_ref, pq_ref, wq_ref, t0)
            k_all = conv(k_ref, pk_ref, wk_ref, t0)
            v_all = conv(v_ref, pv_ref, wv_ref, t0)
            ab = ab_ref[0, pl.ds(t0, span), :]
            g_all = neg_decay_rate * _softplus(ab + dtb_ref[...])
            gcum_all = [_dot_f32(l_incl, g_all[u * c:(u + 1) * c]) for u in range(cu)]
            beta_all = _sigmoid(ab)

            def rows(u):
                return slice(u * c, (u + 1) * c)

            q = [l2n(q_all[rows(u), lanes[j]]) * (dh ** -0.5) for u, j in pairs]
            k = [l2n(k_all[rows(u), lanes[j]]) for u, j in pairs]
            v = [v_all[rows(u), lanes[j]] for u, j in pairs]
            gcum = [pick(lane == h0 + j, gcum_all[u]) for u, j in pairs]
            beta = [pick(lane == h0 + j + GDN_HEADS, beta_all[rows(u)]) for u, j in pairs]
            gcum_row = [jnp.sum(jnp.where(eye, g, 0.0), axis=0, keepdims=True) for g in gcum]
            decay = [jnp.where(incl, jnp.exp(jnp.where(incl, g - gr, 0.0)), 0.0) for g, gr in zip(gcum, gcum_row)]
            kb = [x.astype(BF16) for x in k]
            kk = [_dot_nt(x, x) for x in kb]
            qk = [_dot_nt(x.astype(BF16), y) for x, y in zip(q, kb)]
            a = [jnp.where(strict, b * m * d, 0.0) for b, m, d in zip(beta, kk, decay)]

            inv = [[jnp.where(eye[g * SUBLANES:(g + 1) * SUBLANES], 1.0, 0.0).astype(F32) for g in range(ng)]
                   for _ in hs]
            a_g = [[m[g * SUBLANES:(g + 1) * SUBLANES] for g in range(ng)] for m in a]
            gpb = blk // SUBLANES
            for step in range(blk - 1):
                for b in range(c // blk):
                    g0, jr = divmod(b * blk + step, SUBLANES)
                    for j in hs:
                        row = inv[j][g0][jr:jr + 1, :]
                        for g in range(g0, (b + 1) * gpb):
                            inv[j][g] = inv[j][g] - a_g[j][g][:, b * blk + step:b * blk + step + 1] * row
            inv = [jnp.concatenate(m, axis=0) for m in inv]
            size = blk
            while size < c:
                sh = size.bit_length() - 1
                off_mask = jnp.logical_and(r >> (sh + 1) == col >> (sh + 1), r >> sh != col >> sh)
                off = [jnp.where(off_mask, m, 0.0).astype(BF16) for m in a]
                inv_b = [m.astype(BF16) for m in inv]
                t = [_dot(off[j], inv_b[j]).astype(BF16) for j in hs]
                inv = [inv[j] - _dot(inv_b[j], t[j]) for j in hs]
                size *= 2

            eg = [jnp.exp(g) for g in gcum]
            inv_b = [m.astype(BF16) for m in inv]
            rhs = [jnp.concatenate([beta[j] * v[j], (beta[j] * eg[j]) * k[j]], axis=1) for j in hs]
            rhs_hi = [m.astype(BF16) for m in rhs]
            rhs_lo = [(m - mh.astype(F32)).astype(BF16) for m, mh in zip(rhs, rhs_hi)]
            sol = [_dot(inv_b[j], rhs_hi[j]) + _dot(inv_b[j], rhs_lo[j]) for j in hs]
            wq = [jnp.concatenate([sol[j][:, dh:2 * dh].astype(BF16), (q[j] * eg[j]).astype(BF16)], axis=0) for j in hs]
            qkd = [(qk[j] * decay[j]).astype(BF16) for j in hs]
            g_last = [gcum[j][c - 1:c, :] for j in hs]
            kd = [(k[j] * jnp.exp(g_last[j] - gcum[j])).astype(BF16) for j in hs]

            for u in range(cu):
                tu = pl.multiple_of(t0 + u * c, c)
                ids = [u * hb + j for j in range(hb)]
                s = [s_ref[0, j] for j in range(hb)]
                sb = [x.astype(BF16) for x in s]
                ws = [_dot(wq[i], sb[j]) for j, i in enumerate(ids)]
                vb = [(sol[i][:, 0:dh] - ws[j][0:c]).astype(BF16) for j, i in enumerate(ids)]
                o = [ws[j][c:2 * c] + _dot(qkd[i], vb[j]) for j, i in enumerate(ids)]
                for j, i in enumerate(ids):
                    s_ref[0, j] = s[j] * jnp.exp(g_last[i]) + lax.dot_general(
                        kd[i], vb[j], (((0,), (0,)), ((), ())), preferred_element_type=F32)
                for j, ls in enumerate(lanes):
                    y = _rms(o[j]) * ng_ref[...] * _silu(z_ref[0, pl.ds(tu, c), ls])
                    o_ref[0, pl.ds(tu, c), ls] = y.astype(BF16)
            return carry

        lax.fori_loop(0, n // cu, chunk, 0)

    return kernel


def _gdn_core(qkv, z, ab, cbuf, conv_w, alog_row, dtb_row, norm_g, s0):
    B, T, _ = qkv.shape
    c = min(GDN_CHUNK, T)
    H, dh = GDN_HEADS, GDN_HEAD_DIM
    hb = GDN_HEADS_PER_STEP
    G = H // hb

    def seq(off):
        return pl.BlockSpec((1, T, hb * dh), lambda b, h: (b, 0, off + h))

    def hist(off):
        return pl.BlockSpec((1, GDN_CONV - 1, hb * dh), lambda b, h: (b, 0, off + h))

    def wconv(off):
        return pl.BlockSpec((GDN_CONV, hb * dh), lambda b, h: (0, off + h))

    st_spec = pl.BlockSpec((1, hb, dh, dh), lambda b, h: (b, h, 0, 0))
    cu = GDN_CHUNKS_PER_STEP if (T // c) % GDN_CHUNKS_PER_STEP == 0 else 1
    pad = pltpu.VMEM((cu * c + SUBLANES, hb * dh), F32)
    return pl.pallas_call(
        _make_gdn_kernel(T, c, hb, cu),
        grid=(B, G),
        in_specs=[seq(0), seq(G), seq(2 * G), seq(0),
                  pl.BlockSpec((1, T, LANES), lambda b, h: (b, 0, 0)),
                  hist(0), hist(G), hist(2 * G), wconv(0), wconv(G), wconv(2 * G),
                  _resident((1, LANES)), _resident((1, LANES)), _resident((1, dh)), st_spec],
        out_specs=[seq(0), st_spec],
        out_shape=[jax.ShapeDtypeStruct((B, T, D_MODEL), BF16), jax.ShapeDtypeStruct((B, H, dh, dh), F32)],
        scratch_shapes=[pad, pad, pad],
        compiler_params=_cparams(("parallel", "parallel")),
        name="gdn_core",
    )(qkv, qkv, qkv, z, ab, cbuf, cbuf, cbuf, conv_w, conv_w, conv_w, alog_row, dtb_row, norm_g, s0)


def _make_conv_kernel(T, tq, rsub, lchunk):
    halo = CNV_HALO
    first_hist = halo - (CNV_WIDTH - 1)

    shifted_rows = tq + halo - SUBLANES

    def kernel(u_ref, cb_ref, w_ref, b_ref, lg_ref, lb_ref, y_ref, us_ref, acc_ref):
        t = pl.program_id(1)
        t0 = pl.multiple_of(t * tq, tq)

        @pl.when(t == 0)
        def _():
            us_ref[0, 0:halo, :] = jnp.zeros((halo, D_MODEL), F32)
            us_ref[0, first_hist:halo, :] = cb_ref[0]

        @pl.when(t > 0)
        def _():
            us_ref[0, 0:halo, :] = u_ref[0, pl.ds(t0 - halo, halo), :]

        us_ref[0, halo:halo + tq, :] = u_ref[0, pl.ds(t0, tq), :]
        for b in range(1, SUBLANES):
            us_ref[b, 0:shifted_rows, :] = us_ref[0, b:b + shifted_rows, :]

        for lc in range(D_MODEL // lchunk):
            ls = slice(lc * lchunk, (lc + 1) * lchunk)

            def rows(ri, carry):
                r0 = pl.multiple_of(ri * rsub, rsub)
                acc = b_ref[:, ls]
                for i in range(CNV_WIDTH):
                    a, b = divmod(first_hist + i, SUBLANES)
                    acc = acc + w_ref[i:i + 1, ls] * us_ref[b, pl.ds(r0 + a * SUBLANES, rsub), ls]
                acc_ref[pl.ds(r0, rsub), ls] = acc
                return carry

            lax.fori_loop(0, tq // rsub, rows, 0)

        y = acc_ref[...]
        yc = y - jnp.mean(y, axis=-1, keepdims=True)
        var = jnp.mean(yc * yc, axis=-1, keepdims=True)
        y_ref[0] = _silu(yc * lax.rsqrt(var + EPS) * lg_ref[...] + lb_ref[...]).astype(BF16)

    return kernel


def _conv_core(u, cbuf, dw_w, dw_b, ln_g, ln_b):
    B, T, _ = u.shape
    tq = min(256, T)
    rsub = min(32, tq)
    row = _resident((1, D_MODEL))
    return pl.pallas_call(
        _make_conv_kernel(T, tq, rsub, 512),
        grid=(B, T // tq),
        in_specs=[pl.BlockSpec((1, T, D_MODEL), lambda b, t: (b, 0, 0)),
                  pl.BlockSpec((1, CNV_WIDTH - 1, D_MODEL), lambda b, t: (b, 0, 0)),
                  _resident((CNV_WIDTH, D_MODEL)), row, row, row],
        out_specs=pl.BlockSpec((1, tq, D_MODEL), lambda b, t: (b, t, 0)),
        out_shape=jax.ShapeDtypeStruct((B, T, D_MODEL), BF16),
        scratch_shapes=[pltpu.VMEM((SUBLANES, tq + CNV_HALO, D_MODEL), F32), pltpu.VMEM((tq, D_MODEL), F32)],
        compiler_params=_cparams(("parallel", "arbitrary")),
        name="cnv_core",
    )(u, cbuf, dw_w, dw_b, ln_g, ln_b)


def _row(v):
    return v.astype(F32).reshape(1, -1)


def _pad_lanes(v):
    return jnp.pad(v.astype(F32), (0, LANES - v.shape[0])).reshape(1, LANES)


def _prep_weights(p):
    nff = D_FF // FF_CHUNK
    w = {}
    w['w1'] = [p['ffn_w1'][i].astype(BF16).reshape(D_MODEL, nff, FF_CHUNK).transpose(1, 0, 2) for i in range(DEPTH)]
    w['w2'] = [p['ffn_w2'][i].astype(BF16).reshape(nff, FF_CHUNK, D_MODEL) for i in range(DEPTH)]
    w['sb_qkv'] = p['sb_w_qkv'][0].astype(BF16)
    w['sb_o'] = p['sb_w_o'][0].astype(BF16)
    w['s5'] = _s5_params(p['s5_a_re'][0], p['s5_a_im'][0], p['s5_log_dt'][0], p['s5_b_re'][0], p['s5_b_im'][0],
                         p['s5_c_re'][0], p['s5_c_im'][0])
    w['s5_glu'] = p['s5_w_glu'][0].astype(BF16)
    w_in = p['gdn_w_in'][0]
    w['gdn_in'] = w_in[:, :4 * D_MODEL].astype(BF16)
    w['gdn_ab'] = jnp.pad(w_in[:, 4 * D_MODEL:], ((0, 0), (0, LANES - 2 * GDN_HEADS))).astype(BF16)
    w['gdn_o'] = p['gdn_w_o'][0].astype(BF16)
    w['pw1'] = p['cnv_w_pw1'][0].astype(BF16)
    w['pw2'] = p['cnv_w_pw2'][0].astype(BF16)
    return w


def _trunk(x, mod, sb_k, sb_v, s5_re, s5_im, gdn_s, gdn_cb, cnv_cb, p, w):
    B, T, _ = x.shape
    out = {}
    g_out = _row(p['norm_out_g'])

    def tail(i, x, y, w_pre, glu=False):
        return _tail(x, y, mod[i], w_pre, _row(p['norm_ffn_g'][i]), w['w1'][i], w['w2'][i], g_out,
                     glu=glu, final=(i == DEPTH - 1))

    q, k, v, kb, vb = _qkv_proj(x, mod[0], _row(p['norm_mix_g'][0]), w['sb_qkv'])
    if sb_k is None:
        o = _sb_attention(q, kb, vb)
    else:
        P = sb_k.shape[1]
        o = _sb_attention(q, kb, vb, sb_k.reshape(B, P, D_MODEL).astype(BF16),
                          sb_v.reshape(B, P, D_MODEL).astype(BF16))
    out['sb_k'] = k.reshape(1, B, T, SB_HEADS, SB_HEAD_DIM)
    out['sb_v'] = v.reshape(1, B, T, SB_HEADS, SB_HEAD_DIM)
    x = tail(0, x, o, w['sb_o'])

    wre, wim, a_re, a_im, cw = w['s5']
    yt, s_re, s_im = _s5_scan(x, mod[1][:, 0, :], mod[1][:, 1, :], _row(p['norm_mix_g'][1]),
                              wre, wim, a_re, a_im, cw, _row(p['s5_d'][0]),
                              s5_re.reshape(B, S5_LANES), s5_im.reshape(B, S5_LANES))
    out['s5_re'] = s_re.reshape(1, B, S5_GROUPS, S5_STATE)
    out['s5_im'] = s_im.reshape(1, B, S5_GROUPS, S5_STATE)
    x = tail(1, x, yt, w['s5_glu'], glu=True)

    qkv, z, ab = _gdn_in_proj(x, mod[2], _row(p['norm_mix_g'][2]), w['gdn_in'], w['gdn_ab'])
    o, s_new = _gdn_core(qkv, z, ab, gdn_cb, p['gdn_conv_w'][0].astype(F32), _pad_lanes(p['gdn_a_log'][0]),
                         _pad_lanes(p['gdn_dt_bias'][0]), _row(p['gdn_norm_g'][0]), gdn_s)
    out['gdn'] = s_new[None]
    hist = jnp.concatenate([gdn_cb, qkv], axis=1) if T < GDN_CONV - 1 else qkv
    out['gdn_conv'] = hist[None, :, hist.shape[1] - (GDN_CONV - 1):]
    x = tail(2, x, o, w['gdn_o'])

    u = _pw1_proj(x, mod[3], _row(p['norm_mix_g'][3]), w['pw1'])
    ya = _conv_core(u, cnv_cb, p['cnv_dw_w'][0].astype(F32), _row(p['cnv_dw_b'][0]), _row(p['cnv_ln_g'][0]),
                    _row(p['cnv_ln_b'][0]))
    hist = jnp.concatenate([cnv_cb, u], axis=1) if T < CNV_WIDTH - 1 else u
    out['conv'] = hist[None, :, hist.shape[1] - (CNV_WIDTH - 1):]
    x = tail(3, x, ya, w['pw2'])
    return x, out


def kernel(x_prompt, x_sample, c_prompt, c_sample, cache_sb_k, cache_sb_v, state_s5_re, state_s5_im, state_gdn, state_gdn_conv, state_conv, ada_w, ada_b, norm_mix_g, norm_ffn_g, norm_out_g, ffn_w1, ffn_w2, sb_w_qkv, sb_w_o, s5_a_re, s5_a_im, s5_log_dt, s5_b_re, s5_b_im, s5_c_re, s5_c_im, s5_d, s5_w_glu, gdn_w_in, gdn_conv_w, gdn_a_log, gdn_dt_bias, gdn_norm_g, gdn_w_o, cnv_w_pw1, cnv_dw_w, cnv_dw_b, cnv_ln_g, cnv_ln_b, cnv_w_pw2):
    p = dict(norm_mix_g=norm_mix_g, norm_ffn_g=norm_ffn_g, norm_out_g=norm_out_g, ffn_w1=ffn_w1, ffn_w2=ffn_w2,
             sb_w_qkv=sb_w_qkv, sb_w_o=sb_w_o, s5_a_re=s5_a_re, s5_a_im=s5_a_im, s5_log_dt=s5_log_dt,
             s5_b_re=s5_b_re, s5_b_im=s5_b_im, s5_c_re=s5_c_re, s5_c_im=s5_c_im, s5_d=s5_d, s5_w_glu=s5_w_glu,
             gdn_w_in=gdn_w_in, gdn_conv_w=gdn_conv_w, gdn_a_log=gdn_a_log, gdn_dt_bias=gdn_dt_bias,
             gdn_norm_g=gdn_norm_g, gdn_w_o=gdn_w_o, cnv_w_pw1=cnv_w_pw1, cnv_dw_w=cnv_dw_w, cnv_dw_b=cnv_dw_b,
             cnv_ln_g=cnv_ln_g, cnv_ln_b=cnv_ln_b, cnv_w_pw2=cnv_w_pw2)
    w = _prep_weights(p)
    bp, bs = x_prompt.shape[0], x_sample.shape[0]
    mod = _ada(jnp.concatenate([c_prompt, c_sample], axis=0), ada_w, ada_b)
    mod = mod.reshape(DEPTH, bp + bs, 6, D_MODEL)
    mod_p = [mod[i, :bp] for i in range(DEPTH)]
    mod_s = [mod[i, bp:] for i in range(DEPTH)]

    yp, op = _trunk(
        x_prompt, mod_p, None, None,
        jnp.zeros((bp, S5_GROUPS, S5_STATE), F32), jnp.zeros((bp, S5_GROUPS, S5_STATE), F32),
        jnp.zeros((bp, GDN_HEADS, GDN_HEAD_DIM, GDN_HEAD_DIM), F32),
        jnp.zeros((bp, GDN_CONV - 1, 3 * D_MODEL), F32), jnp.zeros((bp, CNV_WIDTH - 1, D_MODEL), F32), p, w)
    ys, os_ = _trunk(
        x_sample, mod_s, cache_sb_k[0], cache_sb_v[0], state_s5_re[0], state_s5_im[0],
        state_gdn[0], state_gdn_conv[0], state_conv[0], p, w)
    return (yp, ys, op['sb_k'], op['sb_v'], os_['sb_k'], os_['sb_v'],
            op['s5_re'], op['s5_im'], os_['s5_re'], os_['s5_im'],
            op['gdn'], op['gdn_conv'], os_['gdn'], os_['gdn_conv'],
            op['conv'], os_['conv'])
```

```python
import functools

import jax
import jax.numpy as jnp
from jax import lax
from jax.experimental import pallas as pl
from jax.experimental.pallas import tpu as pltpu

F32 = jnp.float32
BF16 = jnp.bfloat16

D_MODEL = 1024
DEPTH = 4
EPS = 1e-6
D_FF = 4 * D_MODEL

SB_HEADS = 16
SB_HEAD_DIM = 64
SB_Q_ROWS = 512
SB_KEY_BLOCK = 256
SB_BLOCKS_PER_STAGE = 2
SB_SKIP = 104.0

S5_GROUPS = 64
S5_GROUP = 16
S5_STATE = 64
S5_LANES = S5_GROUPS * S5_STATE
S5_GROUPS_PER_TILE = 16
S5_TILES = S5_GROUPS // S5_GROUPS_PER_TILE

GDN_HEADS = 8
GDN_HEAD_DIM = 128
GDN_CONV = 4
GDN_CHUNK = 64
GDN_HEADS_PER_STEP = 4
GDN_CHUNKS_PER_STEP = 4
GDN_SOLVE_BLOCK = 16

CNV_WIDTH = 31
CNV_HALO = 32

LANES = 128
SUBLANES = 8
ROW_TILE = 512
FF_CHUNK = 512
FF_UNROLL = 4
V7X_VMEM_BYTES = 64 * 2 ** 20
VMEM_LIMIT = 56 * 2 ** 20


def _cparams(sem):
    return pltpu.CompilerParams(dimension_semantics=sem, vmem_limit_bytes=VMEM_LIMIT)


def _resident(shape):
    nd = len(shape)
    return pl.BlockSpec(shape, lambda *_: (0,) * nd, pipeline_mode=pl.Buffered(1))


def _dot(a, b):
    return jnp.dot(a, b, preferred_element_type=F32)


def _dot_nt(a, b):
    return lax.dot_general(a, b, (((1,), (1,)), ((), ())), preferred_element_type=F32)


def _dot_f32(a, b):
    return jnp.dot(a, b, preferred_element_type=F32, precision=lax.Precision.HIGHEST)


def _sigmoid(x):
    return 1.0 / (1.0 + jnp.exp(-x))


def _silu(x):
    return x * _sigmoid(x)


def _softplus(x):
    return jnp.maximum(x, 0.0) + jnp.log(1.0 + jnp.exp(-jnp.abs(x)))


def _rms(x):
    return x * lax.rsqrt(jnp.mean(x * x, axis=-1, keepdims=True) + EPS)


def _norm_mod(x, g, sc, sh):
    return _rms(x) * g * (1.0 + sc) + sh


def _ada_kernel(c_ref, w_ref, b_ref, o_ref):
    cs = _silu(c_ref[...]).astype(BF16)
    o_ref[0] = _dot(cs, w_ref[0].astype(BF16)) + b_ref[0]


def _ada(c_all, ada_w, ada_b):
    nb = c_all.shape[0]
    ncol = ada_w.shape[-1] // D_MODEL
    return pl.pallas_call(
        _ada_kernel,
        grid=(DEPTH, ncol),
        in_specs=[
            pl.BlockSpec((nb, D_MODEL), lambda l, j: (0, 0)),
            pl.BlockSpec((1, D_MODEL, D_MODEL), lambda l, j: (l, 0, j)),
            pl.BlockSpec((1, 1, D_MODEL), lambda l, j: (l, 0, j)),
        ],
        out_specs=pl.BlockSpec((1, nb, D_MODEL), lambda l, j: (l, 0, j)),
        out_shape=jax.ShapeDtypeStruct((DEPTH, nb, ncol * D_MODEL), F32),
        compiler_params=_cparams(("arbitrary", "arbitrary")),
        name="ada_mod",
    )(c_all, ada_w, ada_b.reshape(DEPTH, 1, -1))


def _row_specs(tm):
    x_spec = pl.BlockSpec((1, tm, D_MODEL), lambda b, t: (b, t, 0))
    mod_spec = pl.BlockSpec((1, 6, D_MODEL), lambda b, t: (b, 0, 0))
    return x_spec, mod_spec


def _mixer_in(x_ref, mod_ref, g_ref):
    return _norm_mod(x_ref[0], g_ref[...], mod_ref[0, 1:2, :], mod_ref[0, 0:1, :]).astype(BF16)


def _qkv_kernel(x_ref, mod_ref, g_ref, w_ref, q_ref, k_ref, v_ref, kb_ref, vb_ref):
    hm = _mixer_in(x_ref, mod_ref, g_ref)
    q_ref[0] = (_dot(hm, w_ref[:, 0:D_MODEL]) * (SB_HEAD_DIM ** -0.5)).astype(BF16)
    k = _dot(hm, w_ref[:, D_MODEL:2 * D_MODEL])
    k_ref[0] = k
    kb_ref[0] = k.astype(BF16)
    v = _dot(hm, w_ref[:, 2 * D_MODEL:3 * D_MODEL])
    v_ref[0] = v
    vb_ref[0] = v.astype(BF16)


def _qkv_proj(x, mod, g, w):
    B, T, _ = x.shape
    tm = min(ROW_TILE, T)
    x_spec, mod_spec = _row_specs(tm)
    out = lambda dt: jax.ShapeDtypeStruct((B, T, D_MODEL), dt)
    return pl.pallas_call(
        _qkv_kernel,
        grid=(B, T // tm),
        in_specs=[x_spec, mod_spec, _resident((1, D_MODEL)), _resident((D_MODEL, 3 * D_MODEL))],
        out_specs=[x_spec] * 5,
        out_shape=[out(BF16), out(F32), out(F32), out(BF16), out(BF16)],
        compiler_params=_cparams(("parallel", "parallel")),
        name="sb_qkv",
    )(x, mod, g, w)


def _gdn_in_kernel(x_ref, mod_ref, g_ref, w_ref, wab_ref, qkv_ref, z_ref, ab_ref):
    hm = _mixer_in(x_ref, mod_ref, g_ref)
    for j in range(3):
        cols = slice(j * D_MODEL, (j + 1) * D_MODEL)
        qkv_ref[0, :, cols] = _dot(hm, w_ref[:, cols])
    z_ref[0] = _dot(hm, w_ref[:, 3 * D_MODEL:4 * D_MODEL])
    ab_ref[0] = _dot(hm, wab_ref[...])


def _gdn_in_proj(x, mod, g, w, wab):
    B, T, _ = x.shape
    tm = min(ROW_TILE, T)
    x_spec, mod_spec = _row_specs(tm)
    return pl.pallas_call(
        _gdn_in_kernel,
        grid=(B, T // tm),
        in_specs=[x_spec, mod_spec, _resident((1, D_MODEL)), _resident((D_MODEL, 4 * D_MODEL)),
                  _resident((D_MODEL, LANES))],
        out_specs=[pl.BlockSpec((1, tm, 3 * D_MODEL), lambda b, t: (b, t, 0)), x_spec,
                   pl.BlockSpec((1, tm, LANES), lambda b, t: (b, t, 0))],
        out_shape=[jax.ShapeDtypeStruct((B, T, 3 * D_MODEL), F32), jax.ShapeDtypeStruct((B, T, D_MODEL), F32),
                   jax.ShapeDtypeStruct((B, T, LANES), F32)],
        compiler_params=_cparams(("parallel", "parallel")),
        name="gdn_in",
    )(x, mod, g, w, wab)


def _pw1_kernel(x_ref, mod_ref, g_ref, w_ref, u_ref):
    hm = _mixer_in(x_ref, mod_ref, g_ref)
    u_ref[0] = _dot(hm, w_ref[:, 0:D_MODEL]) * _sigmoid(_dot(hm, w_ref[:, D_MODEL:2 * D_MODEL]))


def _pw1_proj(x, mod, g, w):
    B, T, _ = x.shape
    tm = min(ROW_TILE, T)
    x_spec, mod_spec = _row_specs(tm)
    return pl.pallas_call(
        _pw1_kernel,
        grid=(B, T // tm),
        in_specs=[x_spec, mod_spec, _resident((1, D_MODEL)), _resident((D_MODEL, 2 * D_MODEL))],
        out_specs=x_spec,
        out_shape=jax.ShapeDtypeStruct((B, T, D_MODEL), F32),
        compiler_params=_cparams(("parallel", "parallel")),
        name="cnv_pw1",
    )(x, mod, g, w)


def _tail_kernel(x_ref, y_ref, mod_ref, wpre_ref, g_ref, w1_ref, w2_ref, gout_ref, o_ref, acc_ref, *, glu, final):
    pre = _dot(y_ref[0], wpre_ref[...])
    if glu:
        pre = pre[:, 0:D_MODEL] * _sigmoid(pre[:, D_MODEL:2 * D_MODEL])
    x1 = x_ref[0] + mod_ref[0, 2:3, :] * pre
    o_ref[0] = x1
    hf = _norm_mod(x1, g_ref[...], mod_ref[0, 4:5, :], mod_ref[0, 3:4, :]).astype(BF16)

    def ff_pair(k):
        a = [jnp.maximum(_dot(hf, w1_ref[k + u]), 0.0) for u in range(FF_UNROLL)]
        p = [_dot((a[u] * a[u]).astype(BF16), w2_ref[k + u]) for u in range(FF_UNROLL)]
        return functools.reduce(lambda s, t: s + t, p)

    acc_ref[...] = ff_pair(0)

    def ff_step(k, carry):
        acc_ref[...] += ff_pair(k * FF_UNROLL)
        return carry

    lax.fori_loop(1, D_FF // (FF_CHUNK * FF_UNROLL), ff_step, 0)
    x2 = o_ref[0] + mod_ref[0, 5:6, :] * acc_ref[...]
    if final:
        x2 = _rms(x2) * gout_ref[...]
    o_ref[0] = x2


def _tail(x, y, mod, w_pre, g_ffn, w1c, w2c, g_out, *, glu, final):
    B, T, _ = x.shape
    tm = min(ROW_TILE, T)
    x_spec, mod_spec = _row_specs(tm)
    ky = y.shape[-1]
    nff = D_FF // FF_CHUNK
    return pl.pallas_call(
        functools.partial(_tail_kernel, glu=glu, final=final),
        grid=(B, T // tm),
        in_specs=[x_spec, pl.BlockSpec((1, tm, ky), lambda b, t: (b, t, 0)), mod_spec,
                  _resident(w_pre.shape), _resident((1, D_MODEL)),
                  _resident((nff, D_MODEL, FF_CHUNK)), _resident((nff, FF_CHUNK, D_MODEL)),
                  _resident((1, D_MODEL))],
        out_specs=x_spec,
        out_shape=jax.ShapeDtypeStruct((B, T, D_MODEL), F32),
        scratch_shapes=[pltpu.VMEM((tm, D_MODEL), F32)],
        compiler_params=_cparams(("parallel", "parallel")),
        name="tail",
    )(x, y, mod, w_pre, g_ffn, w1c, w2c, g_out)


def _make_attn_kernel(tq, past, bq, bkn, bkp):
    nq = tq // bq
    ndiag = bq // bkn
    npast = past // bkp
    hd = SB_HEAD_DIM
    heads = tuple(slice(h * hd, (h + 1) * hd) for h in range(LANES // hd))

    def suffix_matrix(n):
        r = lax.broadcasted_iota(jnp.int32, (n, n), 0)
        c = lax.broadcasted_iota(jnp.int32, (n, n), 1)
        m = (r >= c).astype(BF16)
        return jnp.concatenate([m, m], axis=0)

    def kernel(*refs):
        if past:
            q_ref, k_ref, v_ref, kp_ref, vp_ref, o_ref = refs
        else:
            q_ref, k_ref, v_ref, o_ref = refs
        m_new = suffix_matrix(bkn)
        m_past = suffix_matrix(bkp) if past else None

        def scores(qs, ks, masks, m):
            zs = [_dot_nt(q, k) for q, k in zip(qs, ks)]
            sps = [_softplus(z) for z in zs]
            sps = [sp if mk is None else jnp.where(mk, sp, 0.0) for sp, mk in zip(sps, masks)]
            his = [sp.astype(BF16) for sp in sps]
            los = [(sp - hi.astype(F32)).astype(BF16) for sp, hi in zip(sps, his)]
            return zs, [_dot(jnp.concatenate([hi, lo], axis=1), m) for hi, lo in zip(his, los)]

        def weighted(zs, sufs, carries, vs, masks):
            ws = [jnp.exp(z - suf - c) for z, suf, c in zip(zs, sufs, carries)]
            ws = [w if mk is None else jnp.where(mk, w, 0.0) for w, mk in zip(ws, masks)]
            return [_dot(w.astype(BF16), v) for w, v in zip(ws, vs)]

        def live(carries):
            lowest = carries[0]
            for c in carries[1:]:
                lowest = jnp.minimum(lowest, c)
            return jnp.min(lowest) < SB_SKIP

        def q_group(blocks):
            streams = [(i, h) for i in blocks for h in range(len(heads))]
            qs = [q_ref[0, i * bq:(i + 1) * bq, heads[h]] for i, h in streams]
            carries = [jnp.zeros((bq, 1), F32) for _ in streams]
            accs = [jnp.zeros((bq, hd), F32) for _ in streams]

            kinds = [(d * bkn, d * bkn, True) for d in reversed(range(ndiag))] + [(0, -bkn, False)]
            items = [(s, rs, i * bq + off, masked) for rs, off, masked in kinds
                     for s, (i, _) in enumerate(streams) if masked or i > 0]

            def inside(ref):
                return [ref[0, k0:k0 + bkn, heads[streams[s][1]]] for s, _, k0, _ in items]

            visible = [(lax.broadcasted_iota(jnp.int32, (bq - rs, bkn), 1)
                        < lax.broadcasted_iota(jnp.int32, (bq - rs, bkn), 0)) if masked else None
                       for _, rs, _, masked in items]
            zs, sufs = scores([qs[s][rs:] for s, rs, _, _ in items], inside(k_ref), visible, m_new)
            newer = []
            for (s, rs, _, _), suf in zip(items, sufs):
                newer.append(carries[s][rs:])
                mass = suf[:, 0:1]
                if rs:
                    mass = jnp.concatenate([jnp.zeros((rs, 1), F32), mass], axis=0)
                carries[s] = carries[s] + mass
            for (s, rs, _, _), part in zip(items, weighted(zs, sufs, newer, inside(v_ref), visible)):
                if rs:
                    part = jnp.concatenate([jnp.zeros((rs, hd), F32), part], axis=0)
                accs[s] = accs[s] + part

            def sweep(ss, first, n_blocks, kref, vref, bk, m):
                def cond(st):
                    return jnp.logical_and(st[0] < n_blocks, live(st[1]))

                def body(st):
                    jj, cs, as_ = st
                    k0 = pl.multiple_of((n_blocks - 1 - jj) * bk, bk)
                    unmasked = [None] * len(ss)
                    zs, sufs = scores([qs[s] for s in ss], [kref[0, pl.ds(k0, bk), heads[streams[s][1]]] for s in ss],
                                      unmasked, m)
                    parts = weighted(zs, sufs, cs, [vref[0, pl.ds(k0, bk), heads[streams[s][1]]] for s in ss],
                                     unmasked)
                    return (jj + 1, tuple(c + suf[:, 0:1] for c, suf in zip(cs, sufs)),
                            tuple(a + p for a, p in zip(as_, parts)))

                _, cs, as_ = lax.while_loop(cond, body, (jnp.int32(first), tuple(carries[s] for s in ss),
                                                         tuple(accs[s] for s in ss)))
                for s, c, a in zip(ss, cs, as_):
                    carries[s], accs[s] = c, a

            for i in blocks:
                ss = [s for s, (bi, _) in enumerate(streams) if bi == i]
                if i * ndiag > 1:
                    sweep(ss, 1, i * ndiag, k_ref, v_ref, bkn, m_new)
                if past:
                    sweep(ss, 0, npast, kp_ref, vp_ref, bkp, m_past)
            for s, (i, h) in enumerate(streams):
                o_ref[0, i * bq:(i + 1) * bq, heads[h]] = accs[s].astype(BF16)

        for g0 in range(0, nq, SB_BLOCKS_PER_STAGE):
            q_group(list(range(g0, min(g0 + SB_BLOCKS_PER_STAGE, nq))))

    return kernel


def _sb_attention(q, k, v, k_past=None, v_past=None):
    B, T, _ = q.shape
    past = 0 if k_past is None else k_past.shape[1]
    bq = min(SB_Q_ROWS, T)
    bkn = min(SB_KEY_BLOCK, T)
    seq_spec = pl.BlockSpec((1, T, LANES), lambda b, h: (b, 0, h))
    in_specs = [seq_spec] * 3
    args = [q, k, v]
    if past:
        in_specs += [pl.BlockSpec((1, past, LANES), lambda b, h: (b, 0, h))] * 2
        args += [k_past, v_past]
    return pl.pallas_call(
        _make_attn_kernel(T, past, bq, bkn, SB_KEY_BLOCK),
        grid=(B, D_MODEL // LANES),
        in_specs=in_specs,
        out_specs=seq_spec,
        out_shape=jax.ShapeDtypeStruct((B, T, D_MODEL), BF16),
        compiler_params=_cparams(("parallel", "parallel")),
        name="sb_attn",
    )(*args)


def _gelu_tanh(x):
    return 0.5 * x * (1.0 + jnp.tanh(0.7978845608028654 * (x + 0.044715 * (x * x * x))))


def _make_s5_kernel(c, bb, lane_chunk):
    rows = c * bb
    tile_lanes = S5_GROUPS_PER_TILE * S5_STATE
    tile_cols = S5_GROUPS_PER_TILE * S5_GROUP

    def kernel(x_ref, sh_ref, sc_ref, g_ref, wre_ref, wim_ref, are_ref, aim_ref, cw_ref, d_ref,
               s0r_ref, s0i_ref, y_ref, sr_ref, si_ref, hm_ref, xr_ref, xi_ref):
        @pl.when(pl.program_id(1) == 0)
        def _():
            sr_ref[...] = s0r_ref[...]
            si_ref[...] = s0i_ref[...]

        xt = jnp.swapaxes(x_ref[...], 0, 1)
        hm = _norm_mod(xt, g_ref[...], sc_ref[...][None], sh_ref[...][None])
        hm_ref[...] = hm.reshape(rows, D_MODEL)
        for kt in range(S5_TILES):
            hb = hm_ref[:, kt * tile_cols:(kt + 1) * tile_cols].astype(BF16)
            xr_ref[:, kt * tile_lanes:(kt + 1) * tile_lanes] = _dot(hb, wre_ref[kt])
            xi_ref[:, kt * tile_lanes:(kt + 1) * tile_lanes] = _dot(hb, wim_ref[kt])

        for nt in range(S5_TILES):
            for lc in range(nt * tile_lanes // lane_chunk, (nt + 1) * tile_lanes // lane_chunk):
                ls = slice(lc * lane_chunk, (lc + 1) * lane_chunk)
                a_r = are_ref[:, ls]
                a_i = aim_ref[:, ls]
                s_r = sr_ref[:, ls]
                s_i = si_ref[:, ls]
                for t in range(c):
                    rs = slice(t * bb, (t + 1) * bb)
                    s_r, s_i = (a_r * s_r - a_i * s_i + xr_ref[rs, ls], a_r * s_i + a_i * s_r + xi_ref[rs, ls])
                    xr_ref[rs, ls] = s_r
                    xi_ref[rs, ls] = s_i
                sr_ref[:, ls] = s_r
                si_ref[:, ls] = s_i

            ls = slice(nt * tile_lanes, (nt + 1) * tile_lanes)
            cs = slice(nt * tile_cols, (nt + 1) * tile_cols)
            st = jnp.concatenate([xr_ref[:, ls].astype(BF16), xi_ref[:, ls].astype(BF16)], axis=1)
            yv = _dot(st, cw_ref[nt]) + d_ref[:, cs] * hm_ref[:, cs]
            y_ref[:, :, cs] = jnp.swapaxes(_gelu_tanh(yv).reshape(c, bb, tile_cols), 0, 1).astype(BF16)

    return kernel


def _s5_scan(x, sh, sc, g, wre, wim, a_re, a_im, cw, d, s0r, s0i):
    B, T, _ = x.shape
    bb = min(B, 16)
    c = min(32, T)
    x_spec = pl.BlockSpec((bb, c, D_MODEL), lambda b, t: (b, t, 0))
    b_spec = pl.BlockSpec((bb, D_MODEL), lambda b, t: (b, 0))
    s_spec = pl.BlockSpec((bb, S5_LANES), lambda b, t: (b, 0))
    tile_lanes = S5_GROUPS_PER_TILE * S5_STATE
    tile_cols = S5_GROUPS_PER_TILE * S5_GROUP
    return pl.pallas_call(
        _make_s5_kernel(c, bb, 512),
        grid=(B // bb, T // c),
        in_specs=[x_spec, b_spec, b_spec, _resident((1, D_MODEL)),
                  _resident((S5_TILES, tile_cols, tile_lanes)), _resident((S5_TILES, tile_cols, tile_lanes)),
                  _resident((1, S5_LANES)), _resident((1, S5_LANES)),
                  _resident((S5_TILES, 2 * tile_lanes, tile_cols)), _resident((1, D_MODEL)),
                  s_spec, s_spec],
        out_specs=[x_spec, s_spec, s_spec],
        out_shape=[jax.ShapeDtypeStruct((B, T, D_MODEL), BF16), jax.ShapeDtypeStruct((B, S5_LANES), F32),
                   jax.ShapeDtypeStruct((B, S5_LANES), F32)],
        scratch_shapes=[pltpu.VMEM((c * bb, D_MODEL), F32), pltpu.VMEM((c * bb, S5_LANES), F32),
                        pltpu.VMEM((c * bb, S5_LANES), F32)],
        compiler_params=_cparams(("parallel", "arbitrary")),
        name="s5_scan",
    )(x, sh, sc, g, wre, wim, a_re, a_im, cw, d, s0r, s0i)


def _s5_params(a_re, a_im, log_dt, b_re, b_im, c_re, c_im):
    dt = jnp.exp(log_dt.astype(F32))[:, None]
    ar, ai = a_re.astype(F32), a_im.astype(F32)
    mag = jnp.exp(ar * dt)
    abar_re, abar_im = mag * jnp.cos(ai * dt), mag * jnp.sin(ai * dt)
    den = ar * ar + ai * ai
    f_re = ((abar_re - 1.0) * ar + abar_im * ai) / den
    f_im = (abar_im * ar - (abar_re - 1.0) * ai) / den
    br, bi = b_re.astype(F32), b_im.astype(F32)
    bb_re = f_re[..., None] * br - f_im[..., None] * bi
    bb_im = f_re[..., None] * bi + f_im[..., None] * br
    gt = S5_GROUPS_PER_TILE
    eye = jnp.eye(gt, dtype=F32)

    def pack_in(w):
        w = w.reshape(S5_TILES, gt, S5_STATE, S5_GROUP)
        return jnp.einsum('kgpn,gh->kgnhp', w, eye).reshape(S5_TILES, gt * S5_GROUP, gt * S5_STATE).astype(BF16)

    def pack_out(w):
        w = w.reshape(S5_TILES, gt, S5_GROUP, S5_STATE)
        return jnp.einsum('kgnp,gh->kgphn', w, eye).reshape(S5_TILES, gt * S5_STATE, gt * S5_GROUP)

    cw = jnp.concatenate([pack_out(c_re.astype(F32)), -pack_out(c_im.astype(F32))], axis=1).astype(BF16)
    return (pack_in(bb_re), pack_in(bb_im), abar_re.reshape(1, S5_LANES), abar_im.reshape(1, S5_LANES), cw)


def _make_gdn_kernel(T, c, hb, cu):
    n = T // c
    span = cu * c
    dh = GDN_HEAD_DIM
    ng = c // SUBLANES
    pad0 = SUBLANES
    taps = GDN_CONV
    lanes = tuple(slice(j * dh, (j + 1) * dh) for j in range(hb))
    blk = min(GDN_SOLVE_BLOCK, c)

    def kernel(q_ref, k_ref, v_ref, z_ref, ab_ref, cq_ref, ck_ref, cv_ref, wq_ref, wk_ref, wv_ref,
               alog_ref, dtb_ref, ng_ref, s0_ref, o_ref, s_ref, pq_ref, pk_ref, pv_ref,
               sol_ref, wq_scr, qkd_ref, kd_ref, gl_ref):
        h0 = pl.program_id(1) * hb
        s_ref[0] = s0_ref[0]
        for pref, cref in ((pq_ref, cq_ref), (pk_ref, ck_ref), (pv_ref, cv_ref)):
            pref[0:pad0, :] = jnp.zeros((pad0, hb * dh), F32)
            pref[pad0 - (taps - 1):pad0, :] = cref[0]

        r = lax.broadcasted_iota(jnp.int32, (c, c), 0)
        col = lax.broadcasted_iota(jnp.int32, (c, c), 1)
        incl = r >= col
        strict = r > col
        eye = r == col
        l_incl = incl.astype(F32)
        lane = lax.broadcasted_iota(jnp.int32, (1, LANES), 1)
        neg_decay_rate = -jnp.exp(alog_ref[...])

        def conv(src_ref, pref, w_ref, t0):
            pref[pad0:pad0 + span, :] = src_ref[0, pl.ds(t0, span), :]
            acc = w_ref[0:1, :] * pref[pl.ds(pad0 - (taps - 1), span), :]
            for i in range(1, taps):
                acc = acc + w_ref[i:i + 1, :] * pref[pl.ds(pad0 - (taps - 1) + i, span), :]
            pref[0:pad0, :] = pref[span:span + pad0, :]
            return _silu(acc)

        def pick(sel, m):
            return jnp.sum(jnp.where(sel, m, 0.0), axis=-1, keepdims=True)

        def l2n(x):
            return x * lax.rsqrt(jnp.sum(x * x, axis=-1, keepdims=True) + EPS)

        def row0(ci):
            return ci * span if isinstance(ci, int) else pl.multiple_of(ci * span, span)

        def solve(ci, tick):
            t0 = row0(ci)
            pairs = [(u, j) for u in range(cu) for j in range(hb)]
            hs = range(len(pairs))
            q_all = conv(q_ref, pq_ref, wq_ref, t0)
            k_all = conv(k_ref, pk_ref, wk_ref, t0)
            v_all = conv(v_ref, pv_ref, wv_ref, t0)
            ab = ab_ref[0, pl.ds(t0, span), :]
            g_all = neg_decay_rate * _softplus(ab + dtb_ref[...])
            gcum_all = [_dot_f32(l_incl, g_all[u * c:(u + 1) * c]) for u in range(cu)]
            beta_all = _sigmoid(ab)

            def rows(u):
                return slice(u * c, (u + 1) * c)

            q = [l2n(q_all[rows(u), lanes[j]]) * (dh ** -0.5) for u, j in pairs]
            k = [l2n(k_all[rows(u), lanes[j]]) for u, j in pairs]
            v = [v_all[rows(u), lanes[j]] for u, j in pairs]
            gcum = [pick(lane == h0 + j, gcum_all[u]) for u, j in pairs]
            beta = [pick(lane == h0 + j + GDN_HEADS, beta_all[rows(u)]) for u, j in pairs]
            gcum_row = [jnp.sum(jnp.where(eye, g, 0.0), axis=0, keepdims=True) for g in gcum]
            decay = [jnp.where(incl, jnp.exp(jnp.where(incl, g - gr, 0.0)), 0.0) for g, gr in zip(gcum, gcum_row)]
            kb = [x.astype(BF16) for x in k]
            kk = [_dot_nt(x, x) for x in kb]
            qk = [_dot_nt(x.astype(BF16), y) for x, y in zip(q, kb)]
            a = [jnp.where(strict, b * m * d, 0.0) for b, m, d in zip(beta, kk, decay)]
            tick()

            inv = [[jnp.where(eye[g * SUBLANES:(g + 1) * SUBLANES], 1.0, 0.0).astype(F32) for g in range(ng)]
                   for _ in hs]
            a_g = [[m[g * SUBLANES:(g + 1) * SUBLANES] for g in range(ng)] for m in a]
            gpb = blk // SUBLANES
            for step in range(blk - 1):
                if step % 4 == 3:
                    tick()
                for b in range(c // blk):
                    g0, jr = divmod(b * blk + step, SUBLANES)
                    for j in hs:
                        row = inv[j][g0][jr:jr + 1, :]
                        for g in range(g0, (b + 1) * gpb):
                            inv[j][g] = inv[j][g] - a_g[j][g][:, b * blk + step:b * blk + step + 1] * row
            inv = [jnp.concatenate(m, axis=0) for m in inv]
            size = blk
            while size < c:
                sh = size.bit_length() - 1
                off_mask = jnp.logical_and(r >> (sh + 1) == col >> (sh + 1), r >> sh != col >> sh)
                off = [jnp.where(off_mask, m, 0.0).astype(BF16) for m in a]
                inv_b = [m.astype(BF16) for m in inv]
                t = [_dot(off[j], inv_b[j]).astype(BF16) for j in hs]
                tick()
                inv = [inv[j] - _dot(inv_b[j], t[j]) for j in hs]
                tick()
                size *= 2

            eg = [jnp.exp(g) for g in gcum]
            inv_b = [m.astype(BF16) for m in inv]
            rhs = [jnp.concatenate([beta[j] * v[j], (beta[j] * eg[j]) * k[j]], axis=1) for j in hs]
            rhs_hi = [m.astype(BF16) for m in rhs]
            rhs_lo = [(m - mh.astype(F32)).astype(BF16) for m, mh in zip(rhs, rhs_hi)]
            sol = [_dot(inv_b[j], rhs_hi[j]) + _dot(inv_b[j], rhs_lo[j]) for j in hs]
            wq = [jnp.concatenate([sol[j][:, dh:2 * dh].astype(BF16), (q[j] * eg[j]).astype(BF16)], axis=0) for j in hs]
            qkd = [(qk[j] * decay[j]).astype(BF16) for j in hs]
            g_last = [gcum[j][c - 1:c, :] for j in hs]
            kd = [(k[j] * jnp.exp(g_last[j] - gcum[j])).astype(BF16) for j in hs]
            return sol, wq, qkd, kd, [jnp.exp(g) for g in g_last]

        def park(sol, wq, qkd, kd, decay_last):
            for i in range(cu * hb):
                sol_ref[i] = sol[i]
                wq_scr[i] = wq[i]
                qkd_ref[i] = qkd[i]
                kd_ref[i] = kd[i]
                gl_ref[i] = jnp.broadcast_to(decay_last[i], (SUBLANES, LANES))

        def recur(ci):
            t0 = row0(ci)
            for u in range(cu):
                tu = t0 + u * c if isinstance(t0, int) else pl.multiple_of(t0 + u * c, c)
                ids = [u * hb + j for j in range(hb)]
                s = [s_ref[0, j] for j in range(hb)]
                sb = [x.astype(BF16) for x in s]
                ws = [_dot(wq_scr[i], sb[j]) for j, i in enumerate(ids)]
                yield
                vb = [(sol_ref[i, :, 0:dh] - ws[j][0:c]).astype(BF16) for j, i in enumerate(ids)]
                o = [ws[j][c:2 * c] + _dot(qkd_ref[i], vb[j]) for j, i in enumerate(ids)]
                for j, i in enumerate(ids):
                    s_ref[0, j] = s[j] * gl_ref[i, 0:1, 0:1] + lax.dot_general(
                        kd_ref[i], vb[j], (((0,), (0,)), ((), ())), preferred_element_type=F32)
                for j, ls in enumerate(lanes):
                    y = _rms(o[j]) * ng_ref[...] * _silu(z_ref[0, pl.ds(tu, c), ls])
                    o_ref[0, pl.ds(tu, c), ls] = y.astype(BF16)
                yield

        park(*solve(0, lambda: None))

        def body(ci, carry):
            rec = recur(ci - 1)
            solved = solve(ci, lambda: next(rec, None))
            for _ in rec:
                pass
            park(*solved)
            return carry

        steps = n // cu
        if steps > 1:
            lax.fori_loop(1, steps, body, 0)
        for _ in recur(steps - 1):
            pass

    return kernel


def _gdn_core(qkv, z, ab, cbuf, conv_w, alog_row, dtb_row, norm_g, s0):
    B, T, _ = qkv.shape
    c = min(GDN_CHUNK, T)
    H, dh = GDN_HEADS, GDN_HEAD_DIM
    hb = GDN_HEADS_PER_STEP
    G = H // hb

    def seq(off):
        return pl.BlockSpec((1, T, hb * dh), lambda b, h: (b, 0, off + h))

    def hist(off):
        return pl.BlockSpec((1, GDN_CONV - 1, hb * dh), lambda b, h: (b, 0, off + h))

    def wconv(off):
        return pl.BlockSpec((GDN_CONV, hb * dh), lambda b, h: (0, off + h))

    st_spec = pl.BlockSpec((1, hb, dh, dh), lambda b, h: (b, h, 0, 0))
    cu = GDN_CHUNKS_PER_STEP if (T // c) % GDN_CHUNKS_PER_STEP == 0 else 1
    pad = pltpu.VMEM((cu * c + SUBLANES, hb * dh), F32)
    return pl.pallas_call(
        _make_gdn_kernel(T, c, hb, cu),
        grid=(B, G),
        in_specs=[seq(0), seq(G), seq(2 * G), seq(0),
                  pl.BlockSpec((1, T, LANES), lambda b, h: (b, 0, 0)),
                  hist(0), hist(G), hist(2 * G), wconv(0), wconv(G), wconv(2 * G),
                  _resident((1, LANES)), _resident((1, LANES)), _resident((1, dh)), st_spec],
        out_specs=[seq(0), st_spec],
        out_shape=[jax.ShapeDtypeStruct((B, T, D_MODEL), BF16), jax.ShapeDtypeStruct((B, H, dh, dh), F32)],
        scratch_shapes=[pad, pad, pad,
                        pltpu.VMEM((cu * hb, c, 2 * dh), F32), pltpu.VMEM((cu * hb, 2 * c, dh), BF16),
                        pltpu.VMEM((cu * hb, c, c), BF16), pltpu.VMEM((cu * hb, c, dh), BF16),
                        pltpu.VMEM((cu * hb, SUBLANES, LANES), F32)],
        compiler_params=_cparams(("parallel", "parallel")),
        name="gdn_core",
    )(qkv, qkv, qkv, z, ab, cbuf, cbuf, cbuf, conv_w, conv_w, conv_w, alog_row, dtb_row, norm_g, s0)


def _make_conv_kernel(T, tq, rsub, lchunk):
    halo = CNV_HALO
    first_hist = halo - (CNV_WIDTH - 1)

    shifted_rows = tq + halo - SUBLANES

    def kernel(u_ref, cb_ref, w_ref, b_ref, lg_ref, lb_ref, y_ref, us_ref, acc_ref):
        t = pl.program_id(1)
        t0 = pl.multiple_of(t * tq, tq)

        @pl.when(t == 0)
        def _():
            us_ref[0, 0:halo, :] = jnp.zeros((halo, D_MODEL), F32)
            us_ref[0, first_hist:halo, :] = cb_ref[0]

        @pl.when(t > 0)
        def _():
            us_ref[0, 0:halo, :] = u_ref[0, pl.ds(t0 - halo, halo), :]

        us_ref[0, halo:halo + tq, :] = u_ref[0, pl.ds(t0, tq), :]
        for b in range(1, SUBLANES):
            us_ref[b, 0:shifted_rows, :] = us_ref[0, b:b + shifted_rows, :]

        for lc in range(D_MODEL // lchunk):
            ls = slice(lc * lchunk, (lc + 1) * lchunk)

            def rows(ri, carry):
                r0 = pl.multiple_of(ri * rsub, rsub)
                acc = b_ref[:, ls]
                for i in range(CNV_WIDTH):
                    a, b = divmod(first_hist + i, SUBLANES)
                    acc = acc + w_ref[i:i + 1, ls] * us_ref[b, pl.ds(r0 + a * SUBLANES, rsub), ls]
                acc_ref[pl.ds(r0, rsub), ls] = acc
                return carry

            lax.fori_loop(0, tq // rsub, rows, 0)

        y = acc_ref[...]
        yc = y - jnp.mean(y, axis=-1, keepdims=True)
        var = jnp.mean(yc * yc, axis=-1, keepdims=True)
        y_ref[0] = _silu(yc * lax.rsqrt(var + EPS) * lg_ref[...] + lb_ref[...]).astype(BF16)

    return kernel


def _conv_core(u, cbuf, dw_w, dw_b, ln_g, ln_b):
    B, T, _ = u.shape
    tq = min(256, T)
    rsub = min(32, tq)
    row = _resident((1, D_MODEL))
    return pl.pallas_call(
        _make_conv_kernel(T, tq, rsub, 512),
        grid=(B, T // tq),
        in_specs=[pl.BlockSpec((1, T, D_MODEL), lambda b, t: (b, 0, 0)),
                  pl.BlockSpec((1, CNV_WIDTH - 1, D_MODEL), lambda b, t: (b, 0, 0)),
                  _resident((CNV_WIDTH, D_MODEL)), row, row, row],
        out_specs=pl.BlockSpec((1, tq, D_MODEL), lambda b, t: (b, t, 0)),
        out_shape=jax.ShapeDtypeStruct((B, T, D_MODEL), BF16),
        scratch_shapes=[pltpu.VMEM((SUBLANES, tq + CNV_HALO, D_MODEL), F32), pltpu.VMEM((tq, D_MODEL), F32)],
        compiler_params=_cparams(("parallel", "arbitrary")),
        name="cnv_core",
    )(u, cbuf, dw_w, dw_b, ln_g, ln_b)


def _row(v):
    return v.astype(F32).reshape(1, -1)


def _pad_lanes(v):
    return jnp.pad(v.astype(F32), (0, LANES - v.shape[0])).reshape(1, LANES)


def _prep_weights(p):
    nff = D_FF // FF_CHUNK
    w = {}
    w['w1'] = [p['ffn_w1'][i].astype(BF16).reshape(D_MODEL, nff, FF_CHUNK).transpose(1, 0, 2) for i in range(DEPTH)]
    w['w2'] = [p['ffn_w2'][i].astype(BF16).reshape(nff, FF_CHUNK, D_MODEL) for i in range(DEPTH)]
    w['sb_qkv'] = p['sb_w_qkv'][0].astype(BF16)
    w['sb_o'] = p['sb_w_o'][0].astype(BF16)
    w['s5'] = _s5_params(p['s5_a_re'][0], p['s5_a_im'][0], p['s5_log_dt'][0], p['s5_b_re'][0], p['s5_b_im'][0],
                         p['s5_c_re'][0], p['s5_c_im'][0])
    w['s5_glu'] = p['s5_w_glu'][0].astype(BF16)
    w_in = p['gdn_w_in'][0]
    w['gdn_in'] = w_in[:, :4 * D_MODEL].astype(BF16)
    w['gdn_ab'] = jnp.pad(w_in[:, 4 * D_MODEL:], ((0, 0), (0, LANES - 2 * GDN_HEADS))).astype(BF16)
    w['gdn_o'] = p['gdn_w_o'][0].astype(BF16)
    w['pw1'] = p['cnv_w_pw1'][0].astype(BF16)
    w['pw2'] = p['cnv_w_pw2'][0].astype(BF16)
    return w


def _trunk(x, mod, sb_k, sb_v, s5_re, s5_im, gdn_s, gdn_cb, cnv_cb, p, w):
    B, T, _ = x.shape
    out = {}
    g_out = _row(p['norm_out_g'])

    def tail(i, x, y, w_pre, glu=False):
        return _tail(x, y, mod[i], w_pre, _row(p['norm_ffn_g'][i]), w['w1'][i], w['w2'][i], g_out,
                     glu=glu, final=(i == DEPTH - 1))

    q, k, v, kb, vb = _qkv_proj(x, mod[0], _row(p['norm_mix_g'][0]), w['sb_qkv'])
    if sb_k is None:
        o = _sb_attention(q, kb, vb)
    else:
        P = sb_k.shape[1]
        o = _sb_attention(q, kb, vb, sb_k.reshape(B, P, D_MODEL).astype(BF16),
                          sb_v.reshape(B, P, D_MODEL).astype(BF16))
    out['sb_k'] = k.reshape(1, B, T, SB_HEADS, SB_HEAD_DIM)
    out['sb_v'] = v.reshape(1, B, T, SB_HEADS, SB_HEAD_DIM)
    x = tail(0, x, o, w['sb_o'])

    wre, wim, a_re, a_im, cw = w['s5']
    yt, s_re, s_im = _s5_scan(x, mod[1][:, 0, :], mod[1][:, 1, :], _row(p['norm_mix_g'][1]),
                              wre, wim, a_re, a_im, cw, _row(p['s5_d'][0]),
                              s5_re.reshape(B, S5_LANES), s5_im.reshape(B, S5_LANES))
    out['s5_re'] = s_re.reshape(1, B, S5_GROUPS, S5_STATE)
    out['s5_im'] = s_im.reshape(1, B, S5_GROUPS, S5_STATE)
    x = tail(1, x, yt, w['s5_glu'], glu=True)

    qkv, z, ab = _gdn_in_proj(x, mod[2], _row(p['norm_mix_g'][2]), w['gdn_in'], w['gdn_ab'])
    o, s_new = _gdn_core(qkv, z, ab, gdn_cb, p['gdn_conv_w'][0].astype(F32), _pad_lanes(p['gdn_a_log'][0]),
                         _pad_lanes(p['gdn_dt_bias'][0]), _row(p['gdn_norm_g'][0]), gdn_s)
    out['gdn'] = s_new[None]
    hist = jnp.concatenate([gdn_cb, qkv], axis=1) if T < GDN_CONV - 1 else qkv
    out['gdn_conv'] = hist[None, :, hist.shape[1] - (GDN_CONV - 1):]
    x = tail(2, x, o, w['gdn_o'])

    u = _pw1_proj(x, mod[3], _row(p['norm_mix_g'][3]), w['pw1'])
    ya = _conv_core(u, cnv_cb, p['cnv_dw_w'][0].astype(F32), _row(p['cnv_dw_b'][0]), _row(p['cnv_ln_g'][0]),
                    _row(p['cnv_ln_b'][0]))
    hist = jnp.concatenate([cnv_cb, u], axis=1) if T < CNV_WIDTH - 1 else u
    out['conv'] = hist[None, :, hist.shape[1] - (CNV_WIDTH - 1):]
    x = tail(3, x, ya, w['pw2'])
    return x, out


def kernel(x_prompt, x_sample, c_prompt, c_sample, cache_sb_k, cache_sb_v, state_s5_re, state_s5_im, state_gdn, state_gdn_conv, state_conv, ada_w, ada_b, norm_mix_g, norm_ffn_g, norm_out_g, ffn_w1, ffn_w2, sb_w_qkv, sb_w_o, s5_a_re, s5_a_im, s5_log_dt, s5_b_re, s5_b_im, s5_c_re, s5_c_im, s5_d, s5_w_glu, gdn_w_in, gdn_conv_w, gdn_a_log, gdn_dt_bias, gdn_norm_g, gdn_w_o, cnv_w_pw1, cnv_dw_w, cnv_dw_b, cnv_ln_g, cnv_ln_b, cnv_w_pw2):
    p = dict(norm_mix_g=norm_mix_g, norm_ffn_g=norm_ffn_g, norm_out_g=norm_out_g, ffn_w1=ffn_w1, ffn_w2=ffn_w2,
             sb_w_qkv=sb_w_qkv, sb_w_o=sb_w_o, s5_a_re=s5_a_re, s5_a_im=s5_a_im, s5_log_dt=s5_log_dt,
             s5_b_re=s5_b_re, s5_b_im=s5_b_im, s5_c_re=s5_c_re, s5_c_im=s5_c_im, s5_d=s5_d, s5_w_glu=s5_w_glu,
             gdn_w_in=gdn_w_in, gdn_conv_w=gdn_conv_w, gdn_a_log=gdn_a_log, gdn_dt_bias=gdn_dt_bias,
             gdn_norm_g=gdn_norm_g, gdn_w_o=gdn_w_o, cnv_w_pw1=cnv_w_pw1, cnv_dw_w=cnv_dw_w, cnv_dw_b=cnv_dw_b,
             cnv_ln_g=cnv_ln_g, cnv_ln_b=cnv_ln_b, cnv_w_pw2=cnv_w_pw2)
    w = _prep_weights(p)
    bp, bs = x_prompt.shape[0], x_sample.shape[0]
    mod = _ada(jnp.concatenate([c_prompt, c_sample], axis=0), ada_w, ada_b)
    mod = mod.reshape(DEPTH, bp + bs, 6, D_MODEL)
    mod_p = [mod[i, :bp] for i in range(DEPTH)]
    mod_s = [mod[i, bp:] for i in range(DEPTH)]

    yp, op = _trunk(
        x_prompt, mod_p, None, None,
        jnp.zeros((bp, S5_GROUPS, S5_STATE), F32), jnp.zeros((bp, S5_GROUPS, S5_STATE), F32),
        jnp.zeros((bp, GDN_HEADS, GDN_HEAD_DIM, GDN_HEAD_DIM), F32),
        jnp.zeros((bp, GDN_CONV - 1, 3 * D_MODEL), F32), jnp.zeros((bp, CNV_WIDTH - 1, D_MODEL), F32), p, w)
    ys, os_ = _trunk(
        x_sample, mod_s, cache_sb_k[0], cache_sb_v[0], state_s5_re[0], state_s5_im[0],
        state_gdn[0], state_gdn_conv[0], state_conv[0], p, w)
    return (yp, ys, op['sb_k'], op['sb_v'], os_['sb_k'], os_['sb_v'],
            op['s5_re'], op['s5_im'], os_['s5_re'], os_['s5_im'],
            op['gdn'], op['gdn_conv'], os_['gdn'], os_['gdn_conv'],
            op['conv'], os_['conv'])
```

```python
import functools

import jax
import jax.numpy as jnp
from jax import lax
from jax.experimental import pallas as pl
from jax.experimental.pallas import tpu as pltpu

F32 = jnp.float32
BF16 = jnp.bfloat16

D_MODEL = 1024
DEPTH = 4
EPS = 1e-6
D_FF = 4 * D_MODEL

SB_HEADS = 16
SB_HEAD_DIM = 64
SB_Q_ROWS = 512
SB_KEY_BLOCK = 256
SB_BLOCKS_PER_STAGE = 2
SB_SKIP = 104.0

S5_GROUPS = 64
S5_GROUP = 16
S5_STATE = 64
S5_LANES = S5_GROUPS * S5_STATE
S5_GROUPS_PER_TILE = 16
S5_TILES = S5_GROUPS // S5_GROUPS_PER_TILE

GDN_HEADS = 8
GDN_HEAD_DIM = 128
GDN_CONV = 4
GDN_CHUNK = 64
GDN_HEADS_PER_STEP = 4
GDN_CHUNKS_PER_STEP = 4
GDN_SOLVE_BLOCK = 8

CNV_WIDTH = 31
CNV_HALO = 32

LANES = 128
SUBLANES = 8
ROW_TILE = 512
FF_CHUNK = 512
FF_UNROLL = 4
V7X_VMEM_BYTES = 64 * 2 ** 20
VMEM_LIMIT = 56 * 2 ** 20


def _cparams(sem):
    return pltpu.CompilerParams(dimension_semantics=sem, vmem_limit_bytes=VMEM_LIMIT)


def _resident(shape):
    nd = len(shape)
    return pl.BlockSpec(shape, lambda *_: (0,) * nd, pipeline_mode=pl.Buffered(1))


def _dot(a, b):
    return jnp.dot(a, b, preferred_element_type=F32)


def _dot_nt(a, b):
    return lax.dot_general(a, b, (((1,), (1,)), ((), ())), preferred_element_type=F32)


def _dot_f32(a, b):
    return jnp.dot(a, b, preferred_element_type=F32, precision=lax.Precision.HIGHEST)


def _sigmoid(x):
    return 1.0 / (1.0 + jnp.exp(-x))


def _silu(x):
    return x * _sigmoid(x)


def _softplus(x):
    return jnp.maximum(x, 0.0) + jnp.log(1.0 + jnp.exp(-jnp.abs(x)))


def _rms(x):
    return x * lax.rsqrt(jnp.mean(x * x, axis=-1, keepdims=True) + EPS)


def _norm_mod(x, g, sc, sh):
    return _rms(x) * g * (1.0 + sc) + sh


def _ada_kernel(c_ref, w_ref, b_ref, o_ref):
    cs = _silu(c_ref[...]).astype(BF16)
    o_ref[0] = _dot(cs, w_ref[0].astype(BF16)) + b_ref[0]


def _ada(c_all, ada_w, ada_b):
    nb = c_all.shape[0]
    ncol = ada_w.shape[-1] // D_MODEL
    return pl.pallas_call(
        _ada_kernel,
        grid=(DEPTH, ncol),
        in_specs=[
            pl.BlockSpec((nb, D_MODEL), lambda l, j: (0, 0)),
            pl.BlockSpec((1, D_MODEL, D_MODEL), lambda l, j: (l, 0, j)),
            pl.BlockSpec((1, 1, D_MODEL), lambda l, j: (l, 0, j)),
        ],
        out_specs=pl.BlockSpec((1, nb, D_MODEL), lambda l, j: (l, 0, j)),
        out_shape=jax.ShapeDtypeStruct((DEPTH, nb, ncol * D_MODEL), F32),
        compiler_params=_cparams(("arbitrary", "arbitrary")),
        name="ada_mod",
    )(c_all, ada_w, ada_b.reshape(DEPTH, 1, -1))


def _row_specs(tm):
    x_spec = pl.BlockSpec((1, tm, D_MODEL), lambda b, t: (b, t, 0))
    mod_spec = pl.BlockSpec((1, 6, D_MODEL), lambda b, t: (b, 0, 0))
    return x_spec, mod_spec


def _mixer_in(x_ref, mod_ref, g_ref):
    return _norm_mod(x_ref[0], g_ref[...], mod_ref[0, 1:2, :], mod_ref[0, 0:1, :]).astype(BF16)


def _qkv_kernel(x_ref, mod_ref, g_ref, w_ref, q_ref, k_ref, v_ref, kb_ref, vb_ref):
    hm = _mixer_in(x_ref, mod_ref, g_ref)
    q_ref[0] = (_dot(hm, w_ref[:, 0:D_MODEL]) * (SB_HEAD_DIM ** -0.5)).astype(BF16)
    k = _dot(hm, w_ref[:, D_MODEL:2 * D_MODEL])
    k_ref[0] = k
    kb_ref[0] = k.astype(BF16)
    v = _dot(hm, w_ref[:, 2 * D_MODEL:3 * D_MODEL])
    v_ref[0] = v
    vb_ref[0] = v.astype(BF16)


def _qkv_proj(x, mod, g, w):
    B, T, _ = x.shape
    tm = min(ROW_TILE, T)
    x_spec, mod_spec = _row_specs(tm)
    out = lambda dt: jax.ShapeDtypeStruct((B, T, D_MODEL), dt)
    return pl.pallas_call(
        _qkv_kernel,
        grid=(B, T // tm),
        in_specs=[x_spec, mod_spec, _resident((1, D_MODEL)), _resident((D_MODEL, 3 * D_MODEL))],
        out_specs=[x_spec] * 5,
        out_shape=[out(BF16), out(F32), out(F32), out(BF16), out(BF16)],
        compiler_params=_cparams(("parallel", "parallel")),
        name="sb_qkv",
    )(x, mod, g, w)


def _gdn_in_kernel(x_ref, mod_ref, g_ref, w_ref, wab_ref, cb_ref, cw_ref, qkv_ref, z_ref, ab_ref, hist_ref, pad_ref):
    tm = x_ref.shape[1]
    taps = GDN_CONV
    first = SUBLANES - (taps - 1)

    @pl.when(pl.program_id(1) == 0)
    def _():
        hist_ref[0] = jnp.zeros((SUBLANES, 3 * D_MODEL), F32)
        hist_ref[0, first:SUBLANES, :] = cb_ref[0]

    hm = _mixer_in(x_ref, mod_ref, g_ref)
    pad_ref[0:SUBLANES, :] = hist_ref[0]

    def project(j):
        cols = slice(j * D_MODEL, (j + 1) * D_MODEL)
        pad_ref[SUBLANES:SUBLANES + tm, cols] = _dot(hm, w_ref[:, cols])
        hist_ref[0, :, cols] = pad_ref[tm:tm + SUBLANES, cols]

    project(0)
    for j in range(3):
        if j < 2:
            project(j + 1)
        else:
            z_ref[0] = _dot(hm, w_ref[:, 3 * D_MODEL:4 * D_MODEL])
            ab_ref[0] = _dot(hm, wab_ref[...])
        for h in range(GDN_HEADS):
            cols = slice(j * D_MODEL + h * GDN_HEAD_DIM, j * D_MODEL + (h + 1) * GDN_HEAD_DIM)
            acc = cw_ref[0:1, cols] * pad_ref[first:first + tm, cols]
            for i in range(1, taps):
                acc = acc + cw_ref[i:i + 1, cols] * pad_ref[first + i:first + i + tm, cols]
            y = _silu(acc)
            if j < 2:
                y = y * lax.rsqrt(jnp.sum(y * y, axis=-1, keepdims=True) + EPS)
            if j == 0:
                y = y * (GDN_HEAD_DIM ** -0.5)
            qkv_ref[0, :, cols] = y


def _gdn_in_proj(x, mod, g, w, wab, cbuf, conv_w):
    B, T, _ = x.shape
    tm = min(ROW_TILE, T)
    assert T >= SUBLANES
    x_spec, mod_spec = _row_specs(tm)
    wide = 3 * D_MODEL
    return pl.pallas_call(
        _gdn_in_kernel,
        grid=(B, T // tm),
        in_specs=[x_spec, mod_spec, _resident((1, D_MODEL)), _resident((D_MODEL, 4 * D_MODEL)),
                  _resident((D_MODEL, LANES)), pl.BlockSpec((1, GDN_CONV - 1, wide), lambda b, t: (b, 0, 0)),
                  _resident((GDN_CONV, wide))],
        out_specs=[pl.BlockSpec((1, tm, wide), lambda b, t: (b, t, 0)), x_spec,
                   pl.BlockSpec((1, tm, LANES), lambda b, t: (b, t, 0)),
                   pl.BlockSpec((1, SUBLANES, wide), lambda b, t: (b, 0, 0))],
        out_shape=[jax.ShapeDtypeStruct((B, T, wide), F32), jax.ShapeDtypeStruct((B, T, D_MODEL), F32),
                   jax.ShapeDtypeStruct((B, T, LANES), F32), jax.ShapeDtypeStruct((B, SUBLANES, wide), F32)],
        scratch_shapes=[pltpu.VMEM((tm + SUBLANES, wide), F32)],
        compiler_params=_cparams(("parallel", "arbitrary")),
        name="gdn_in",
    )(x, mod, g, w, wab, cbuf, conv_w)


def _pw1_kernel(x_ref, mod_ref, g_ref, w_ref, u_ref):
    hm = _mixer_in(x_ref, mod_ref, g_ref)
    u_ref[0] = _dot(hm, w_ref[:, 0:D_MODEL]) * _sigmoid(_dot(hm, w_ref[:, D_MODEL:2 * D_MODEL]))


def _pw1_proj(x, mod, g, w):
    B, T, _ = x.shape
    tm = min(ROW_TILE, T)
    x_spec, mod_spec = _row_specs(tm)
    return pl.pallas_call(
        _pw1_kernel,
        grid=(B, T // tm),
        in_specs=[x_spec, mod_spec, _resident((1, D_MODEL)), _resident((D_MODEL, 2 * D_MODEL))],
        out_specs=x_spec,
        out_shape=jax.ShapeDtypeStruct((B, T, D_MODEL), F32),
        compiler_params=_cparams(("parallel", "parallel")),
        name="cnv_pw1",
    )(x, mod, g, w)


def _tail_kernel(x_ref, y_ref, mod_ref, wpre_ref, g_ref, w1_ref, w2_ref, gout_ref, o_ref, *, glu, final):
    bt, tm, _ = x_ref.shape
    rows = bt * tm

    def mod_rows(i):
        if bt == 1:
            return mod_ref[0, i:i + 1, :]
        return jnp.broadcast_to(mod_ref[:, i:i + 1, :], (bt, tm, D_MODEL)).reshape(rows, D_MODEL)

    pre = _dot(y_ref[...].reshape(rows, y_ref.shape[-1]), wpre_ref[...])
    if glu:
        pre = pre[:, 0:D_MODEL] * _sigmoid(pre[:, D_MODEL:2 * D_MODEL])
    x1 = x_ref[...].reshape(rows, D_MODEL) + mod_rows(2) * pre
    hf = _norm_mod(x1, g_ref[...], mod_rows(4), mod_rows(3)).astype(BF16)

    acc = None
    for k0 in range(0, D_FF, FF_CHUNK * FF_UNROLL):
        cols = [slice(k0 + u * FF_CHUNK, k0 + (u + 1) * FF_CHUNK) for u in range(FF_UNROLL)]
        a = [jnp.maximum(_dot(hf, w1_ref[:, cs]), 0.0) for cs in cols]
        for a_u, cs in zip(a, cols):
            p = _dot((a_u * a_u).astype(BF16), w2_ref[cs, :])
            acc = p if acc is None else acc + p
    x2 = x1 + mod_rows(5) * acc
    if final:
        x2 = _rms(x2) * gout_ref[...]
    o_ref[...] = x2.reshape(bt, tm, D_MODEL)


def _tail(x, y, mod, w_pre, g_ffn, w1, w2, g_out, *, glu, final):
    B, T, _ = x.shape
    tm = min(ROW_TILE, T)
    bt = max(1, min(B, ROW_TILE // tm))
    assert B % bt == 0
    ky = y.shape[-1]

    def rows_spec(n):
        return pl.BlockSpec((bt, tm, n), lambda b, t: (b, t, 0))

    return pl.pallas_call(
        functools.partial(_tail_kernel, glu=glu, final=final),
        grid=(B // bt, T // tm),
        in_specs=[rows_spec(D_MODEL), rows_spec(ky), pl.BlockSpec((bt, 6, D_MODEL), lambda b, t: (b, 0, 0)),
                  _resident(w_pre.shape), _resident((1, D_MODEL)),
                  _resident((D_MODEL, D_FF)), _resident((D_FF, D_MODEL)), _resident((1, D_MODEL))],
        out_specs=rows_spec(D_MODEL),
        out_shape=jax.ShapeDtypeStruct((B, T, D_MODEL), F32),
        compiler_params=_cparams(("parallel", "parallel")),
        name="tail",
    )(x, y, mod, w_pre, g_ffn, w1, w2, g_out)


def _make_attn_kernel(tq, past, bq, bkn, bkp):
    nq = tq // bq
    ndiag = bq // bkn
    npast = past // bkp
    hd = SB_HEAD_DIM
    heads = tuple(slice(h * hd, (h + 1) * hd) for h in range(LANES // hd))

    def suffix_matrix(n):
        r = lax.broadcasted_iota(jnp.int32, (n, n), 0)
        c = lax.broadcasted_iota(jnp.int32, (n, n), 1)
        m = (r >= c).astype(BF16)
        return jnp.concatenate([m, m], axis=0)

    def kernel(*refs):
        if past:
            q_ref, k_ref, v_ref, kp_ref, vp_ref, o_ref = refs
        else:
            q_ref, k_ref, v_ref, o_ref = refs
        m_new = suffix_matrix(bkn)
        m_past = suffix_matrix(bkp) if past else None

        def scores(qs, ks, masks, m):
            zs = [_dot_nt(q, k) for q, k in zip(qs, ks)]
            sps = [_softplus(z) for z in zs]
            sps = [sp if mk is None else jnp.where(mk, sp, 0.0) for sp, mk in zip(sps, masks)]
            his = [sp.astype(BF16) for sp in sps]
            los = [(sp - hi.astype(F32)).astype(BF16) for sp, hi in zip(sps, his)]
            return zs, [_dot(jnp.concatenate([hi, lo], axis=1), m) for hi, lo in zip(his, los)]

        def weighted(zs, sufs, carries, vs, masks):
            ws = [jnp.exp(z - suf - c) for z, suf, c in zip(zs, sufs, carries)]
            ws = [w if mk is None else jnp.where(mk, w, 0.0) for w, mk in zip(ws, masks)]
            return [_dot(w.astype(BF16), v) for w, v in zip(ws, vs)]

        def live(carries):
            lowest = carries[0]
            for c in carries[1:]:
                lowest = jnp.minimum(lowest, c)
            return jnp.min(lowest) < SB_SKIP

        def q_group(blocks):
            streams = [(i, h) for i in blocks for h in range(len(heads))]
            qs = [q_ref[0, i * bq:(i + 1) * bq, heads[h]] for i, h in streams]
            carries = [jnp.zeros((bq, 1), F32) for _ in streams]
            accs = [jnp.zeros((bq, hd), F32) for _ in streams]

            kinds = [(d * bkn, d * bkn, True) for d in reversed(range(ndiag))] + [(0, -bkn, False)]
            items = [(s, rs, i * bq + off, masked) for rs, off, masked in kinds
                     for s, (i, _) in enumerate(streams) if masked or i > 0]

            def inside(ref):
                return [ref[0, k0:k0 + bkn, heads[streams[s][1]]] for s, _, k0, _ in items]

            visible = [(lax.broadcasted_iota(jnp.int32, (bq - rs, bkn), 1)
                        < lax.broadcasted_iota(jnp.int32, (bq - rs, bkn), 0)) if masked else None
                       for _, rs, _, masked in items]
            zs, sufs = scores([qs[s][rs:] for s, rs, _, _ in items], inside(k_ref), visible, m_new)
            newer = []
            for (s, rs, _, _), suf in zip(items, sufs):
                newer.append(carries[s][rs:])
                mass = suf[:, 0:1]
                if rs:
                    mass = jnp.concatenate([jnp.zeros((rs, 1), F32), mass], axis=0)
                carries[s] = carries[s] + mass
            for (s, rs, _, _), part in zip(items, weighted(zs, sufs, newer, inside(v_ref), visible)):
                if rs:
                    part = jnp.concatenate([jnp.zeros((rs, hd), F32), part], axis=0)
                accs[s] = accs[s] + part

            def sweep(ss, first, n_blocks, kref, vref, bk, m):
                def cond(st):
                    return jnp.logical_and(st[0] < n_blocks, live(st[1]))

                def body(st):
                    jj, cs, as_ = st
                    k0 = pl.multiple_of((n_blocks - 1 - jj) * bk, bk)
                    unmasked = [None] * len(ss)
                    zs, sufs = scores([qs[s] for s in ss], [kref[0, pl.ds(k0, bk), heads[streams[s][1]]] for s in ss],
                                      unmasked, m)
                    parts = weighted(zs, sufs, cs, [vref[0, pl.ds(k0, bk), heads[streams[s][1]]] for s in ss],
                                     unmasked)
                    return (jj + 1, tuple(c + suf[:, 0:1] for c, suf in zip(cs, sufs)),
                            tuple(a + p for a, p in zip(as_, parts)))

                _, cs, as_ = lax.while_loop(cond, body, (jnp.int32(first), tuple(carries[s] for s in ss),
                                                         tuple(accs[s] for s in ss)))
                for s, c, a in zip(ss, cs, as_):
                    carries[s], accs[s] = c, a

            for i in blocks:
                ss = [s for s, (bi, _) in enumerate(streams) if bi == i]
                if i * ndiag > 1:
                    sweep(ss, 1, i * ndiag, k_ref, v_ref, bkn, m_new)
                if past:
                    sweep(ss, 0, npast, kp_ref, vp_ref, bkp, m_past)
            for s, (i, h) in enumerate(streams):
                o_ref[0, i * bq:(i + 1) * bq, heads[h]] = accs[s].astype(BF16)

        for g0 in range(0, nq, SB_BLOCKS_PER_STAGE):
            q_group(list(range(g0, min(g0 + SB_BLOCKS_PER_STAGE, nq))))

    return kernel


def _sb_attention(q, k, v, k_past=None, v_past=None):
    B, T, _ = q.shape
    past = 0 if k_past is None else k_past.shape[1]
    bq = min(SB_Q_ROWS, T)
    bkn = min(SB_KEY_BLOCK, T)
    seq_spec = pl.BlockSpec((1, T, LANES), lambda b, h: (b, 0, h))
    in_specs = [seq_spec] * 3
    args = [q, k, v]
    if past:
        in_specs += [pl.BlockSpec((1, past, LANES), lambda b, h: (b, 0, h))] * 2
        args += [k_past, v_past]
    return pl.pallas_call(
        _make_attn_kernel(T, past, bq, bkn, SB_KEY_BLOCK),
        grid=(B, D_MODEL // LANES),
        in_specs=in_specs,
        out_specs=seq_spec,
        out_shape=jax.ShapeDtypeStruct((B, T, D_MODEL), BF16),
        compiler_params=_cparams(("parallel", "parallel")),
        name="sb_attn",
    )(*args)


def _gelu_tanh(x):
    return 0.5 * x * (1.0 + jnp.tanh(0.7978845608028654 * (x + 0.044715 * (x * x * x))))


def _make_s5_kernel(c, bb, lane_chunk):
    rows = c * bb
    tile_lanes = S5_GROUPS_PER_TILE * S5_STATE
    tile_cols = S5_GROUPS_PER_TILE * S5_GROUP

    def kernel(x_ref, sh_ref, sc_ref, g_ref, wre_ref, wim_ref, are_ref, aim_ref, cw_ref, d_ref,
               s0r_ref, s0i_ref, y_ref, sr_ref, si_ref, hm_ref, xr_ref, xi_ref):
        @pl.when(pl.program_id(1) == 0)
        def _():
            sr_ref[...] = s0r_ref[...]
            si_ref[...] = s0i_ref[...]

        xt = jnp.swapaxes(x_ref[...], 0, 1)
        hm = _norm_mod(xt, g_ref[...], sc_ref[...][None], sh_ref[...][None])
        hm_ref[...] = hm.reshape(rows, D_MODEL)
        for kt in range(S5_TILES):
            hb = hm_ref[:, kt * tile_cols:(kt + 1) * tile_cols].astype(BF16)
            xr_ref[:, kt * tile_lanes:(kt + 1) * tile_lanes] = _dot(hb, wre_ref[kt])
            xi_ref[:, kt * tile_lanes:(kt + 1) * tile_lanes] = _dot(hb, wim_ref[kt])

        for nt in range(S5_TILES):
            for lc in range(nt * tile_lanes // lane_chunk, (nt + 1) * tile_lanes // lane_chunk):
                ls = slice(lc * lane_chunk, (lc + 1) * lane_chunk)
                a_r = are_ref[:, ls]
                a_i = aim_ref[:, ls]
                s_r = sr_ref[:, ls]
                s_i = si_ref[:, ls]
                for t in range(c):
                    rs = slice(t * bb, (t + 1) * bb)
                    s_r, s_i = (a_r * s_r - a_i * s_i + xr_ref[rs, ls], a_r * s_i + a_i * s_r + xi_ref[rs, ls])
                    xr_ref[rs, ls] = s_r
                    xi_ref[rs, ls] = s_i
                sr_ref[:, ls] = s_r
                si_ref[:, ls] = s_i

            ls = slice(nt * tile_lanes, (nt + 1) * tile_lanes)
            cs = slice(nt * tile_cols, (nt + 1) * tile_cols)
            st = jnp.concatenate([xr_ref[:, ls].astype(BF16), xi_ref[:, ls].astype(BF16)], axis=1)
            yv = _dot(st, cw_ref[nt]) + d_ref[:, cs] * hm_ref[:, cs]
            y_ref[:, :, cs] = jnp.swapaxes(_gelu_tanh(yv).reshape(c, bb, tile_cols), 0, 1).astype(BF16)

    return kernel


def _s5_scan(x, sh, sc, g, wre, wim, a_re, a_im, cw, d, s0r, s0i):
    B, T, _ = x.shape
    bb = min(B, 16)
    c = min(32, T)
    x_spec = pl.BlockSpec((bb, c, D_MODEL), lambda b, t: (b, t, 0))
    b_spec = pl.BlockSpec((bb, D_MODEL), lambda b, t: (b, 0))
    s_spec = pl.BlockSpec((bb, S5_LANES), lambda b, t: (b, 0))
    tile_lanes = S5_GROUPS_PER_TILE * S5_STATE
    tile_cols = S5_GROUPS_PER_TILE * S5_GROUP
    return pl.pallas_call(
        _make_s5_kernel(c, bb, 512),
        grid=(B // bb, T // c),
        in_specs=[x_spec, b_spec, b_spec, _resident((1, D_MODEL)),
                  _resident((S5_TILES, tile_cols, tile_lanes)), _resident((S5_TILES, tile_cols, tile_lanes)),
                  _resident((1, S5_LANES)), _resident((1, S5_LANES)),
                  _resident((S5_TILES, 2 * tile_lanes, tile_cols)), _resident((1, D_MODEL)),
                  s_spec, s_spec],
        out_specs=[x_spec, s_spec, s_spec],
        out_shape=[jax.ShapeDtypeStruct((B, T, D_MODEL), BF16), jax.ShapeDtypeStruct((B, S5_LANES), F32),
                   jax.ShapeDtypeStruct((B, S5_LANES), F32)],
        scratch_shapes=[pltpu.VMEM((c * bb, D_MODEL), F32), pltpu.VMEM((c * bb, S5_LANES), F32),
                        pltpu.VMEM((c * bb, S5_LANES), F32)],
        compiler_params=_cparams(("parallel", "arbitrary")),
        name="s5_scan",
    )(x, sh, sc, g, wre, wim, a_re, a_im, cw, d, s0r, s0i)


def _s5_params(a_re, a_im, log_dt, b_re, b_im, c_re, c_im):
    dt = jnp.exp(log_dt.astype(F32))[:, None]
    ar, ai = a_re.astype(F32), a_im.astype(F32)
    mag = jnp.exp(ar * dt)
    abar_re, abar_im = mag * jnp.cos(ai * dt), mag * jnp.sin(ai * dt)
    den = ar * ar + ai * ai
    f_re = ((abar_re - 1.0) * ar + abar_im * ai) / den
    f_im = (abar_im * ar - (abar_re - 1.0) * ai) / den
    br, bi = b_re.astype(F32), b_im.astype(F32)
    bb_re = f_re[..., None] * br - f_im[..., None] * bi
    bb_im = f_re[..., None] * bi + f_im[..., None] * br
    gt = S5_GROUPS_PER_TILE
    eye = jnp.eye(gt, dtype=F32)

    def pack_in(w):
        w = w.reshape(S5_TILES, gt, S5_STATE, S5_GROUP)
        return jnp.einsum('kgpn,gh->kgnhp', w, eye).reshape(S5_TILES, gt * S5_GROUP, gt * S5_STATE).astype(BF16)

    def pack_out(w):
        w = w.reshape(S5_TILES, gt, S5_GROUP, S5_STATE)
        return jnp.einsum('kgnp,gh->kgphn', w, eye).reshape(S5_TILES, gt * S5_STATE, gt * S5_GROUP)

    cw = jnp.concatenate([pack_out(c_re.astype(F32)), -pack_out(c_im.astype(F32))], axis=1).astype(BF16)
    return (pack_in(bb_re), pack_in(bb_im), abar_re.reshape(1, S5_LANES), abar_im.reshape(1, S5_LANES), cw)


def _make_gdn_kernel(T, c, hb, cu):
    n = T // c
    span = cu * c
    dh = GDN_HEAD_DIM
    ng = c // SUBLANES
    lanes = tuple(slice(j * dh, (j + 1) * dh) for j in range(hb))
    blk = min(GDN_SOLVE_BLOCK, c)

    def kernel(q_ref, k_ref, v_ref, z_ref, ab_ref, alog_ref, dtb_ref, ng_ref, s0_ref, o_ref, s_ref,
               sol_ref, wq_scr, qkd_ref, kd_ref, gl_ref):
        h0 = pl.program_id(1) * hb
        s_ref[0] = s0_ref[0]

        r = lax.broadcasted_iota(jnp.int32, (c, c), 0)
        col = lax.broadcasted_iota(jnp.int32, (c, c), 1)
        incl = r >= col
        strict = r > col
        eye = r == col
        l_incl = incl.astype(F32)
        lane = lax.broadcasted_iota(jnp.int32, (1, LANES), 1)
        neg_decay_rate = -jnp.exp(alog_ref[...])

        def pick(sel, m):
            return jnp.sum(jnp.where(sel, m, 0.0), axis=-1, keepdims=True)

        def row0(ci):
            return ci * span if isinstance(ci, int) else pl.multiple_of(ci * span, span)

        def solve(ci, tick):
            t0 = row0(ci)
            pairs = [(u, j) for u in range(cu) for j in range(hb)]
            hs = range(len(pairs))
            ab = ab_ref[0, pl.ds(t0, span), :]
            g_all = neg_decay_rate * _softplus(ab + dtb_ref[...])
            gcum_all = [_dot_f32(l_incl, g_all[u * c:(u + 1) * c]) for u in range(cu)]
            beta_all = _sigmoid(ab)

            def rows(u):
                return slice(u * c, (u + 1) * c)

            def chunk_of(ref, u, j):
                return ref[0, pl.ds(t0 + u * c, c) if isinstance(t0, int) else pl.ds(pl.multiple_of(t0 + u * c, c), c),
                           lanes[j]]

            q = [chunk_of(q_ref, u, j) for u, j in pairs]
            k = [chunk_of(k_ref, u, j) for u, j in pairs]
            v = [chunk_of(v_ref, u, j) for u, j in pairs]
            gcum = [pick(lane == h0 + j, gcum_all[u]) for u, j in pairs]
            beta = [pick(lane == h0 + j + GDN_HEADS, beta_all[rows(u)]) for u, j in pairs]
            gcum_row = [jnp.sum(jnp.where(eye, g, 0.0), axis=0, keepdims=True) for g in gcum]
            decay = [jnp.where(incl, jnp.exp(jnp.where(incl, g - gr, 0.0)), 0.0) for g, gr in zip(gcum, gcum_row)]
            kb = [x.astype(BF16) for x in k]
            kk = [_dot_nt(x, x) for x in kb]
            qk = [_dot_nt(x.astype(BF16), y) for x, y in zip(q, kb)]
            a = [jnp.where(strict, b * m * d, 0.0) for b, m, d in zip(beta, kk, decay)]
            tick()

            inv = [[jnp.where(eye[g * SUBLANES:(g + 1) * SUBLANES], 1.0, 0.0).astype(F32) for g in range(ng)]
                   for _ in hs]
            a_g = [[m[g * SUBLANES:(g + 1) * SUBLANES] for g in range(ng)] for m in a]
            gpb = blk // SUBLANES
            for step in range(blk - 1):
                if step % 4 == 3:
                    tick()
                for b in range(c // blk):
                    g0, jr = divmod(b * blk + step, SUBLANES)
                    for j in hs:
                        row = inv[j][g0][jr:jr + 1, :]
                        for g in range(g0, (b + 1) * gpb):
                            inv[j][g] = inv[j][g] - a_g[j][g][:, b * blk + step:b * blk + step + 1] * row
            inv = [jnp.concatenate(m, axis=0) for m in inv]
            size = blk
            while size < c:
                sh = size.bit_length() - 1
                off_mask = jnp.logical_and(r >> (sh + 1) == col >> (sh + 1), r >> sh != col >> sh)
                off = [jnp.where(off_mask, m, 0.0).astype(BF16) for m in a]
                inv_b = [m.astype(BF16) for m in inv]
                t = [_dot(off[j], inv_b[j]).astype(BF16) for j in hs]
                tick()
                inv = [inv[j] - _dot(inv_b[j], t[j]) for j in hs]
                tick()
                size *= 2

            eg = [jnp.exp(g) for g in gcum]
            inv_b = [m.astype(BF16) for m in inv]
            rhs = [jnp.concatenate([beta[j] * v[j], (beta[j] * eg[j]) * k[j]], axis=1) for j in hs]
            rhs_hi = [m.astype(BF16) for m in rhs]
            rhs_lo = [(m - mh.astype(F32)).astype(BF16) for m, mh in zip(rhs, rhs_hi)]
            sol = [_dot(inv_b[j], rhs_hi[j]) + _dot(inv_b[j], rhs_lo[j]) for j in hs]
            wq = [jnp.concatenate([sol[j][:, dh:2 * dh].astype(BF16), (q[j] * eg[j]).astype(BF16)], axis=0) for j in hs]
            qkd = [(qk[j] * decay[j]).astype(BF16) for j in hs]
            g_last = [gcum[j][c - 1:c, :] for j in hs]
            kd = [(k[j] * jnp.exp(g_last[j] - gcum[j])).astype(BF16) for j in hs]
            return sol, wq, qkd, kd, [jnp.exp(g) for g in g_last]

        def park(sol, wq, qkd, kd, decay_last):
            for i in range(cu * hb):
                sol_ref[i] = sol[i]
                wq_scr[i] = wq[i]
                qkd_ref[i] = qkd[i]
                kd_ref[i] = kd[i]
                gl_ref[i] = jnp.broadcast_to(decay_last[i], (SUBLANES, LANES))

        def recur(ci):
            t0 = row0(ci)
            for u in range(cu):
                tu = t0 + u * c if isinstance(t0, int) else pl.multiple_of(t0 + u * c, c)
                ids = [u * hb + j for j in range(hb)]
                s = [s_ref[0, j] for j in range(hb)]
                sb = [x.astype(BF16) for x in s]
                ws = [_dot(wq_scr[i], sb[j]) for j, i in enumerate(ids)]
                yield
                vb = [(sol_ref[i, :, 0:dh] - ws[j][0:c]).astype(BF16) for j, i in enumerate(ids)]
                o = [ws[j][c:2 * c] + _dot(qkd_ref[i], vb[j]) for j, i in enumerate(ids)]
                for j, i in enumerate(ids):
                    s_ref[0, j] = s[j] * gl_ref[i, 0:1, 0:1] + lax.dot_general(
                        kd_ref[i], vb[j], (((0,), (0,)), ((), ())), preferred_element_type=F32)
                for j, ls in enumerate(lanes):
                    y = _rms(o[j]) * ng_ref[...] * _silu(z_ref[0, pl.ds(tu, c), ls])
                    o_ref[0, pl.ds(tu, c), ls] = y.astype(BF16)
                yield

        park(*solve(0, lambda: None))

        def body(ci, carry):
            rec = recur(ci - 1)
            solved = solve(ci, lambda: next(rec, None))
            for _ in rec:
                pass
            park(*solved)
            return carry

        steps = n // cu
        if steps > 1:
            lax.fori_loop(1, steps, body, 0)
        for _ in recur(steps - 1):
            pass

    return kernel


def _gdn_core(qkv, z, ab, alog_row, dtb_row, norm_g, s0):
    B, T, _ = qkv.shape
    c = min(GDN_CHUNK, T)
    H, dh = GDN_HEADS, GDN_HEAD_DIM
    hb = GDN_HEADS_PER_STEP
    G = H // hb

    def seq(off):
        return pl.BlockSpec((1, T, hb * dh), lambda b, h: (b, 0, off + h))

    st_spec = pl.BlockSpec((1, hb, dh, dh), lambda b, h: (b, h, 0, 0))
    cu = GDN_CHUNKS_PER_STEP if (T // c) % GDN_CHUNKS_PER_STEP == 0 else 1
    return pl.pallas_call(
        _make_gdn_kernel(T, c, hb, cu),
        grid=(B, G),
        in_specs=[seq(0), seq(G), seq(2 * G), seq(0),
                  pl.BlockSpec((1, T, LANES), lambda b, h: (b, 0, 0)),
                  _resident((1, LANES)), _resident((1, LANES)), _resident((1, dh)), st_spec],
        out_specs=[seq(0), st_spec],
        out_shape=[jax.ShapeDtypeStruct((B, T, D_MODEL), BF16), jax.ShapeDtypeStruct((B, H, dh, dh), F32)],
        scratch_shapes=[pltpu.VMEM((cu * hb, c, 2 * dh), F32), pltpu.VMEM((cu * hb, 2 * c, dh), BF16),
                        pltpu.VMEM((cu * hb, c, c), BF16), pltpu.VMEM((cu * hb, c, dh), BF16),
                        pltpu.VMEM((cu * hb, SUBLANES, LANES), F32)],
        compiler_params=_cparams(("parallel", "parallel")),
        name="gdn_core",
    )(qkv, qkv, qkv, z, ab, alog_row, dtb_row, norm_g, s0)


def _make_conv_kernel(T, tq, rsub, lchunk):
    halo = CNV_HALO
    first_hist = halo - (CNV_WIDTH - 1)

    shifted_rows = tq + halo - SUBLANES

    def kernel(u_ref, cb_ref, w_ref, b_ref, lg_ref, lb_ref, y_ref, us_ref, acc_ref):
        t = pl.program_id(1)
        t0 = pl.multiple_of(t * tq, tq)

        @pl.when(t == 0)
        def _():
            us_ref[0, 0:halo, :] = jnp.zeros((halo, D_MODEL), F32)
            us_ref[0, first_hist:halo, :] = cb_ref[0]

        @pl.when(t > 0)
        def _():
            us_ref[0, 0:halo, :] = u_ref[0, pl.ds(t0 - halo, halo), :]

        us_ref[0, halo:halo + tq, :] = u_ref[0, pl.ds(t0, tq), :]
        for b in range(1, SUBLANES):
            us_ref[b, 0:shifted_rows, :] = us_ref[0, b:b + shifted_rows, :]

        for lc in range(D_MODEL // lchunk):
            ls = slice(lc * lchunk, (lc + 1) * lchunk)

            def rows(ri, carry):
                r0 = pl.multiple_of(ri * rsub, rsub)
                acc = b_ref[:, ls]
                for i in range(CNV_WIDTH):
                    a, b = divmod(first_hist + i, SUBLANES)
                    acc = acc + w_ref[i:i + 1, ls] * us_ref[b, pl.ds(r0 + a * SUBLANES, rsub), ls]
                acc_ref[pl.ds(r0, rsub), ls] = acc
                return carry

            lax.fori_loop(0, tq // rsub, rows, 0)

        y = acc_ref[...]
        yc = y - jnp.mean(y, axis=-1, keepdims=True)
        var = jnp.mean(yc * yc, axis=-1, keepdims=True)
        y_ref[0] = _silu(yc * lax.rsqrt(var + EPS) * lg_ref[...] + lb_ref[...]).astype(BF16)

    return kernel


def _conv_core(u, cbuf, dw_w, dw_b, ln_g, ln_b):
    B, T, _ = u.shape
    tq = min(256, T)
    rsub = min(32, tq)
    row = _resident((1, D_MODEL))
    return pl.pallas_call(
        _make_conv_kernel(T, tq, rsub, 512),
        grid=(B, T // tq),
        in_specs=[pl.BlockSpec((1, T, D_MODEL), lambda b, t: (b, 0, 0)),
                  pl.BlockSpec((1, CNV_WIDTH - 1, D_MODEL), lambda b, t: (b, 0, 0)),
                  _resident((CNV_WIDTH, D_MODEL)), row, row, row],
        out_specs=pl.BlockSpec((1, tq, D_MODEL), lambda b, t: (b, t, 0)),
        out_shape=jax.ShapeDtypeStruct((B, T, D_MODEL), BF16),
        scratch_shapes=[pltpu.VMEM((SUBLANES, tq + CNV_HALO, D_MODEL), F32), pltpu.VMEM((tq, D_MODEL), F32)],
        compiler_params=_cparams(("parallel", "arbitrary")),
        name="cnv_core",
    )(u, cbuf, dw_w, dw_b, ln_g, ln_b)


def _row(v):
    return v.astype(F32).reshape(1, -1)


def _pad_lanes(v):
    return jnp.pad(v.astype(F32), (0, LANES - v.shape[0])).reshape(1, LANES)


def _prep_weights(p):
    w = {}
    w['w1'] = [p['ffn_w1'][i].astype(BF16) for i in range(DEPTH)]
    w['w2'] = [p['ffn_w2'][i].astype(BF16) for i in range(DEPTH)]
    w['sb_qkv'] = p['sb_w_qkv'][0].astype(BF16)
    w['sb_o'] = p['sb_w_o'][0].astype(BF16)
    w['s5'] = _s5_params(p['s5_a_re'][0], p['s5_a_im'][0], p['s5_log_dt'][0], p['s5_b_re'][0], p['s5_b_im'][0],
                         p['s5_c_re'][0], p['s5_c_im'][0])
    w['s5_glu'] = p['s5_w_glu'][0].astype(BF16)
    w_in = p['gdn_w_in'][0]
    w['gdn_in'] = w_in[:, :4 * D_MODEL].astype(BF16)
    w['gdn_ab'] = jnp.pad(w_in[:, 4 * D_MODEL:], ((0, 0), (0, LANES - 2 * GDN_HEADS))).astype(BF16)
    w['gdn_o'] = p['gdn_w_o'][0].astype(BF16)
    w['pw1'] = p['cnv_w_pw1'][0].astype(BF16)
    w['pw2'] = p['cnv_w_pw2'][0].astype(BF16)
    return w


def _trunk(x, mod, sb_k, sb_v, s5_re, s5_im, gdn_s, gdn_cb, cnv_cb, p, w):
    B, T, _ = x.shape
    out = {}
    g_out = _row(p['norm_out_g'])

    def tail(i, x, y, w_pre, glu=False):
        return _tail(x, y, mod[i], w_pre, _row(p['norm_ffn_g'][i]), w['w1'][i], w['w2'][i], g_out,
                     glu=glu, final=(i == DEPTH - 1))

    q, k, v, kb, vb = _qkv_proj(x, mod[0], _row(p['norm_mix_g'][0]), w['sb_qkv'])
    if sb_k is None:
        o = _sb_attention(q, kb, vb)
    else:
        P = sb_k.shape[1]
        o = _sb_attention(q, kb, vb, sb_k.reshape(B, P, D_MODEL).astype(BF16),
                          sb_v.reshape(B, P, D_MODEL).astype(BF16))
    out['sb_k'] = k.reshape(1, B, T, SB_HEADS, SB_HEAD_DIM)
    out['sb_v'] = v.reshape(1, B, T, SB_HEADS, SB_HEAD_DIM)
    x = tail(0, x, o, w['sb_o'])

    wre, wim, a_re, a_im, cw = w['s5']
    yt, s_re, s_im = _s5_scan(x, mod[1][:, 0, :], mod[1][:, 1, :], _row(p['norm_mix_g'][1]),
                              wre, wim, a_re, a_im, cw, _row(p['s5_d'][0]),
                              s5_re.reshape(B, S5_LANES), s5_im.reshape(B, S5_LANES))
    out['s5_re'] = s_re.reshape(1, B, S5_GROUPS, S5_STATE)
    out['s5_im'] = s_im.reshape(1, B, S5_GROUPS, S5_STATE)
    x = tail(1, x, yt, w['s5_glu'], glu=True)

    qkv, z, ab, hist = _gdn_in_proj(x, mod[2], _row(p['norm_mix_g'][2]), w['gdn_in'], w['gdn_ab'], gdn_cb,
                                    p['gdn_conv_w'][0].astype(F32))
    o, s_new = _gdn_core(qkv, z, ab, _pad_lanes(p['gdn_a_log'][0]), _pad_lanes(p['gdn_dt_bias'][0]),
                         _row(p['gdn_norm_g'][0]), gdn_s)
    out['gdn'] = s_new[None]
    out['gdn_conv'] = hist[None, :, SUBLANES - (GDN_CONV - 1):]
    x = tail(2, x, o, w['gdn_o'])

    u = _pw1_proj(x, mod[3], _row(p['norm_mix_g'][3]), w['pw1'])
    ya = _conv_core(u, cnv_cb, p['cnv_dw_w'][0].astype(F32), _row(p['cnv_dw_b'][0]), _row(p['cnv_ln_g'][0]),
                    _row(p['cnv_ln_b'][0]))
    hist = jnp.concatenate([cnv_cb, u], axis=1) if T < CNV_WIDTH - 1 else u
    out['conv'] = hist[None, :, hist.shape[1] - (CNV_WIDTH - 1):]
    x = tail(3, x, ya, w['pw2'])
    return x, out


def kernel(x_prompt, x_sample, c_prompt, c_sample, cache_sb_k, cache_sb_v, state_s5_re, state_s5_im, state_gdn, state_gdn_conv, state_conv, ada_w, ada_b, norm_mix_g, norm_ffn_g, norm_out_g, ffn_w1, ffn_w2, sb_w_qkv, sb_w_o, s5_a_re, s5_a_im, s5_log_dt, s5_b_re, s5_b_im, s5_c_re, s5_c_im, s5_d, s5_w_glu, gdn_w_in, gdn_conv_w, gdn_a_log, gdn_dt_bias, gdn_norm_g, gdn_w_o, cnv_w_pw1, cnv_dw_w, cnv_dw_b, cnv_ln_g, cnv_ln_b, cnv_w_pw2):
    p = dict(norm_mix_g=norm_mix_g, norm_ffn_g=norm_ffn_g, norm_out_g=norm_out_g, ffn_w1=ffn_w1, ffn_w2=ffn_w2,
             sb_w_qkv=sb_w_qkv, sb_w_o=sb_w_o, s5_a_re=s5_a_re, s5_a_im=s5_a_im, s5_log_dt=s5_log_dt,
             s5_b_re=s5_b_re, s5_b_im=s5_b_im, s5_c_re=s5_c_re, s5_c_im=s5_c_im, s5_d=s5_d, s5_w_glu=s5_w_glu,
             gdn_w_in=gdn_w_in, gdn_conv_w=gdn_conv_w, gdn_a_log=gdn_a_log, gdn_dt_bias=gdn_dt_bias,
             gdn_norm_g=gdn_norm_g, gdn_w_o=gdn_w_o, cnv_w_pw1=cnv_w_pw1, cnv_dw_w=cnv_dw_w, cnv_dw_b=cnv_dw_b,
             cnv_ln_g=cnv_ln_g, cnv_ln_b=cnv_ln_b, cnv_w_pw2=cnv_w_pw2)
    w = _prep_weights(p)
    bp, bs = x_prompt.shape[0], x_sample.shape[0]
    mod = _ada(jnp.concatenate([c_prompt, c_sample], axis=0), ada_w, ada_b)
    mod = mod.reshape(DEPTH, bp + bs, 6, D_MODEL)
    mod_p = [mod[i, :bp] for i in range(DEPTH)]
    mod_s = [mod[i, bp:] for i in range(DEPTH)]

    yp, op = _trunk(
        x_prompt, mod_p, None, None,
        jnp.zeros((bp, S5_GROUPS, S5_STATE), F32), jnp.zeros((bp, S5_GROUPS, S5_STATE), F32),
        jnp.zeros((bp, GDN_HEADS, GDN_HEAD_DIM, GDN_HEAD_DIM), F32),
        jnp.zeros((bp, GDN_CONV - 1, 3 * D_MODEL), F32), jnp.zeros((bp, CNV_WIDTH - 1, D_MODEL), F32), p, w)
    ys, os_ = _trunk(
        x_sample, mod_s, cache_sb_k[0], cache_sb_v[0], state_s5_re[0], state_s5_im[0],
        state_gdn[0], state_gdn_conv[0], state_conv[0], p, w)
    return (yp, ys, op['sb_k'], op['sb_v'], os_['sb_k'], os_['sb_v'],
            op['s5_re'], op['s5_im'], os_['s5_re'], os_['s5_im'],
            op['gdn'], op['gdn_conv'], os_['gdn'], os_['gdn_conv'],
            op['conv'], os_['conv'])
```

```python
import functools

import jax
import jax.numpy as jnp
from jax import lax
from jax.experimental import pallas as pl
from jax.experimental.pallas import tpu as pltpu

F32 = jnp.float32
BF16 = jnp.bfloat16

D_MODEL = 1024
DEPTH = 4
EPS = 1e-6
D_FF = 4 * D_MODEL

SB_HEADS = 16
SB_HEAD_DIM = 64
SB_Q_ROWS = 512
SB_KEY_BLOCK = 256
SB_BLOCKS_PER_STAGE = 2
SB_SKIP = 104.0

S5_GROUPS = 64
S5_GROUP = 16
S5_STATE = 64
S5_LANES = S5_GROUPS * S5_STATE
S5_GROUPS_PER_TILE = 16
S5_TILES = S5_GROUPS // S5_GROUPS_PER_TILE

GDN_HEADS = 8
GDN_HEAD_DIM = 128
GDN_CONV = 4
GDN_CHUNK = 64
GDN_HEADS_PER_STEP = 4
GDN_CHUNKS_PER_STEP = 4
GDN_PAD_PITCH = 2
GDN_SOLVE_BLOCK = 8

CNV_WIDTH = 31
CNV_HALO = 32

LANES = 128
SUBLANES = 8
ROW_TILE = 512
FF_CHUNK = 512
FF_UNROLL = 4
V7X_VMEM_BYTES = 64 * 2 ** 20
VMEM_LIMIT = 56 * 2 ** 20


def _cparams(sem):
    return pltpu.CompilerParams(dimension_semantics=sem, vmem_limit_bytes=VMEM_LIMIT)


def _resident(shape):
    nd = len(shape)
    return pl.BlockSpec(shape, lambda *_: (0,) * nd, pipeline_mode=pl.Buffered(1))


def _dot(a, b):
    return jnp.dot(a, b, preferred_element_type=F32)


def _dot_nt(a, b):
    return lax.dot_general(a, b, (((1,), (1,)), ((), ())), preferred_element_type=F32)


def _dot_f32(a, b):
    return jnp.dot(a, b, preferred_element_type=F32, precision=lax.Precision.HIGHEST)


def _sigmoid(x):
    return 1.0 / (1.0 + jnp.exp(-x))


def _silu(x):
    return x * _sigmoid(x)


def _softplus(x):
    return jnp.maximum(x, 0.0) + jnp.log(1.0 + jnp.exp(-jnp.abs(x)))


def _rms(x):
    return x * lax.rsqrt(jnp.mean(x * x, axis=-1, keepdims=True) + EPS)


def _norm_mod(x, g, sc, sh):
    return _rms(x) * g * (1.0 + sc) + sh


def _ada_kernel(c_ref, w_ref, b_ref, o_ref):
    cs = _silu(c_ref[...]).astype(BF16)
    o_ref[0] = _dot(cs, w_ref[0].astype(BF16)) + b_ref[0]


def _ada(c_all, ada_w, ada_b):
    nb = c_all.shape[0]
    ncol = ada_w.shape[-1] // D_MODEL
    return pl.pallas_call(
        _ada_kernel,
        grid=(DEPTH, ncol),
        in_specs=[
            pl.BlockSpec((nb, D_MODEL), lambda l, j: (0, 0)),
            pl.BlockSpec((1, D_MODEL, D_MODEL), lambda l, j: (l, 0, j)),
            pl.BlockSpec((1, 1, D_MODEL), lambda l, j: (l, 0, j)),
        ],
        out_specs=pl.BlockSpec((1, nb, D_MODEL), lambda l, j: (l, 0, j)),
        out_shape=jax.ShapeDtypeStruct((DEPTH, nb, ncol * D_MODEL), F32),
        compiler_params=_cparams(("arbitrary", "arbitrary")),
        name="ada_mod",
    )(c_all, ada_w, ada_b.reshape(DEPTH, 1, -1))


def _row_specs(tm):
    x_spec = pl.BlockSpec((1, tm, D_MODEL), lambda b, t: (b, t, 0))
    mod_spec = pl.BlockSpec((1, 6, D_MODEL), lambda b, t: (b, 0, 0))
    return x_spec, mod_spec


def _mixer_in(x_ref, mod_ref, g_ref):
    return _norm_mod(x_ref[0], g_ref[...], mod_ref[0, 1:2, :], mod_ref[0, 0:1, :]).astype(BF16)


def _qkv_kernel(x_ref, mod_ref, g_ref, w_ref, q_ref, k_ref, v_ref, kb_ref, vb_ref):
    hm = _mixer_in(x_ref, mod_ref, g_ref)
    q_ref[0] = (_dot(hm, w_ref[:, 0:D_MODEL]) * (SB_HEAD_DIM ** -0.5)).astype(BF16)
    k = _dot(hm, w_ref[:, D_MODEL:2 * D_MODEL])
    k_ref[0] = k
    kb_ref[0] = k.astype(BF16)
    v = _dot(hm, w_ref[:, 2 * D_MODEL:3 * D_MODEL])
    v_ref[0] = v
    vb_ref[0] = v.astype(BF16)


def _qkv_proj(x, mod, g, w):
    B, T, _ = x.shape
    tm = min(ROW_TILE, T)
    x_spec, mod_spec = _row_specs(tm)
    out = lambda dt: jax.ShapeDtypeStruct((B, T, D_MODEL), dt)
    return pl.pallas_call(
        _qkv_kernel,
        grid=(B, T // tm),
        in_specs=[x_spec, mod_spec, _resident((1, D_MODEL)), _resident((D_MODEL, 3 * D_MODEL))],
        out_specs=[x_spec] * 5,
        out_shape=[out(BF16), out(F32), out(F32), out(BF16), out(BF16)],
        compiler_params=_cparams(("parallel", "parallel")),
        name="sb_qkv",
    )(x, mod, g, w)


def _gdn_in_kernel(x_ref, mod_ref, g_ref, w_ref, wab_ref, cb_ref, cw_ref, qkv_ref, z_ref, ab_ref, hist_ref, pad_ref):
    tm = x_ref.shape[1]
    taps = GDN_CONV
    first = SUBLANES - (taps - 1)

    @pl.when(pl.program_id(1) == 0)
    def _():
        hist_ref[0] = jnp.zeros((SUBLANES, 3 * D_MODEL), F32)
        hist_ref[0, first:SUBLANES, :] = cb_ref[0]

    hm = _mixer_in(x_ref, mod_ref, g_ref)

    def rows(r0, n):
        return pl.ds(GDN_PAD_PITCH * r0, n, stride=GDN_PAD_PITCH)

    def slab_cols(s):
        return slice(s * LANES, (s + 1) * LANES)

    nslab = D_MODEL // LANES
    for s in range(3 * nslab):
        pad_ref[s, rows(0, SUBLANES), :] = hist_ref[0, :, slab_cols(s)]

    def project(j):
        raw = _dot(hm, w_ref[:, j * D_MODEL:(j + 1) * D_MODEL])
        for s in range(nslab):
            pad_ref[j * nslab + s, rows(SUBLANES, tm), :] = raw[:, slab_cols(s)]
            hist_ref[0, :, slab_cols(j * nslab + s)] = raw[tm - SUBLANES:tm, slab_cols(s)]

    project(0)
    for j in range(3):
        if j < 2:
            project(j + 1)
        else:
            z_ref[0] = _dot(hm, w_ref[:, 3 * D_MODEL:4 * D_MODEL])
            ab_ref[0] = _dot(hm, wab_ref[...])
        for h in range(GDN_HEADS):
            s = j * nslab + h
            cols = slab_cols(s)
            acc = cw_ref[0:1, cols] * pad_ref[s, rows(first, tm), :]
            for i in range(1, taps):
                acc = acc + cw_ref[i:i + 1, cols] * pad_ref[s, rows(first + i, tm), :]
            y = _silu(acc)
            if j < 2:
                y = y * lax.rsqrt(jnp.sum(y * y, axis=-1, keepdims=True) + EPS)
            if j == 0:
                y = y * (GDN_HEAD_DIM ** -0.5)
            qkv_ref[0, :, cols] = y


def _gdn_in_proj(x, mod, g, w, wab, cbuf, conv_w):
    B, T, _ = x.shape
    tm = min(ROW_TILE, T)
    assert T >= SUBLANES
    x_spec, mod_spec = _row_specs(tm)
    wide = 3 * D_MODEL
    return pl.pallas_call(
        _gdn_in_kernel,
        grid=(B, T // tm),
        in_specs=[x_spec, mod_spec, _resident((1, D_MODEL)), _resident((D_MODEL, 4 * D_MODEL)),
                  _resident((D_MODEL, LANES)), pl.BlockSpec((1, GDN_CONV - 1, wide), lambda b, t: (b, 0, 0)),
                  _resident((GDN_CONV, wide))],
        out_specs=[pl.BlockSpec((1, tm, wide), lambda b, t: (b, t, 0)), x_spec,
                   pl.BlockSpec((1, tm, LANES), lambda b, t: (b, t, 0)),
                   pl.BlockSpec((1, SUBLANES, wide), lambda b, t: (b, 0, 0))],
        out_shape=[jax.ShapeDtypeStruct((B, T, wide), F32), jax.ShapeDtypeStruct((B, T, D_MODEL), F32),
                   jax.ShapeDtypeStruct((B, T, LANES), F32), jax.ShapeDtypeStruct((B, SUBLANES, wide), F32)],
        scratch_shapes=[pltpu.VMEM((wide // LANES, GDN_PAD_PITCH * (tm + SUBLANES), LANES), F32)],
        compiler_params=_cparams(("parallel", "arbitrary")),
        name="gdn_in",
    )(x, mod, g, w, wab, cbuf, conv_w)


def _pw1_kernel(x_ref, mod_ref, g_ref, w_ref, u_ref):
    hm = _mixer_in(x_ref, mod_ref, g_ref)
    u_ref[0] = _dot(hm, w_ref[:, 0:D_MODEL]) * _sigmoid(_dot(hm, w_ref[:, D_MODEL:2 * D_MODEL]))


def _pw1_proj(x, mod, g, w):
    B, T, _ = x.shape
    tm = min(ROW_TILE, T)
    x_spec, mod_spec = _row_specs(tm)
    return pl.pallas_call(
        _pw1_kernel,
        grid=(B, T // tm),
        in_specs=[x_spec, mod_spec, _resident((1, D_MODEL)), _resident((D_MODEL, 2 * D_MODEL))],
        out_specs=x_spec,
        out_shape=jax.ShapeDtypeStruct((B, T, D_MODEL), F32),
        compiler_params=_cparams(("parallel", "parallel")),
        name="cnv_pw1",
    )(x, mod, g, w)


def _tail_kernel(x_ref, y_ref, mod_ref, wpre_ref, g_ref, w1_ref, w2_ref, gout_ref, o_ref, *, glu, final):
    bt, tm, _ = x_ref.shape
    rows = bt * tm

    def mod_rows(i):
        if bt == 1:
            return mod_ref[0, i:i + 1, :]
        return jnp.broadcast_to(mod_ref[:, i:i + 1, :], (bt, tm, D_MODEL)).reshape(rows, D_MODEL)

    pre = _dot(y_ref[...].reshape(rows, y_ref.shape[-1]), wpre_ref[...])
    if glu:
        pre = pre[:, 0:D_MODEL] * _sigmoid(pre[:, D_MODEL:2 * D_MODEL])
    x1 = x_ref[...].reshape(rows, D_MODEL) + mod_rows(2) * pre
    hf = _norm_mod(x1, g_ref[...], mod_rows(4), mod_rows(3)).astype(BF16)

    acc = None
    for k0 in range(0, D_FF, FF_CHUNK * FF_UNROLL):
        cols = [slice(k0 + u * FF_CHUNK, k0 + (u + 1) * FF_CHUNK) for u in range(FF_UNROLL)]
        a = [jnp.maximum(_dot(hf, w1_ref[:, cs]), 0.0) for cs in cols]
        for a_u, cs in zip(a, cols):
            p = _dot((a_u * a_u).astype(BF16), w2_ref[cs, :])
            acc = p if acc is None else acc + p
    x2 = x1 + mod_rows(5) * acc
    if final:
        x2 = _rms(x2) * gout_ref[...]
    o_ref[...] = x2.reshape(bt, tm, D_MODEL)


def _tail(x, y, mod, w_pre, g_ffn, w1, w2, g_out, *, glu, final):
    B, T, _ = x.shape
    tm = min(ROW_TILE, T)
    bt = max(1, min(B, ROW_TILE // tm))
    assert B % bt == 0
    ky = y.shape[-1]

    def rows_spec(n):
        return pl.BlockSpec((bt, tm, n), lambda b, t: (b, t, 0))

    return pl.pallas_call(
        functools.partial(_tail_kernel, glu=glu, final=final),
        grid=(B // bt, T // tm),
        in_specs=[rows_spec(D_MODEL), rows_spec(ky), pl.BlockSpec((bt, 6, D_MODEL), lambda b, t: (b, 0, 0)),
                  _resident(w_pre.shape), _resident((1, D_MODEL)),
                  _resident((D_MODEL, D_FF)), _resident((D_FF, D_MODEL)), _resident((1, D_MODEL))],
        out_specs=rows_spec(D_MODEL),
        out_shape=jax.ShapeDtypeStruct((B, T, D_MODEL), F32),
        compiler_params=_cparams(("parallel", "parallel")),
        name="tail",
    )(x, y, mod, w_pre, g_ffn, w1, w2, g_out)


def _make_attn_kernel(tq, past, bq, bkn, bkp):
    nq = tq // bq
    ndiag = bq // bkn
    npast = past // bkp
    hd = SB_HEAD_DIM
    heads = tuple(slice(h * hd, (h + 1) * hd) for h in range(LANES // hd))

    def suffix_matrix(n):
        r = lax.broadcasted_iota(jnp.int32, (n, n), 0)
        c = lax.broadcasted_iota(jnp.int32, (n, n), 1)
        m = (r >= c).astype(BF16)
        return jnp.concatenate([m, m], axis=0)

    def kernel(*refs):
        if past:
            q_ref, k_ref, v_ref, kp_ref, vp_ref, o_ref = refs
        else:
            q_ref, k_ref, v_ref, o_ref = refs
        m_new = suffix_matrix(bkn)
        m_past = suffix_matrix(bkp) if past else None

        def scores(qs, ks, masks, m):
            zs = [_dot_nt(q, k) for q, k in zip(qs, ks)]
            sps = [_softplus(z) for z in zs]
            sps = [sp if mk is None else jnp.where(mk, sp, 0.0) for sp, mk in zip(sps, masks)]
            his = [sp.astype(BF16) for sp in sps]
            los = [(sp - hi.astype(F32)).astype(BF16) for sp, hi in zip(sps, his)]
            return zs, [_dot(jnp.concatenate([hi, lo], axis=1), m) for hi, lo in zip(his, los)]

        def weighted(zs, sufs, carries, vs, masks):
            ws = [jnp.exp(z - suf - c) for z, suf, c in zip(zs, sufs, carries)]
            ws = [w if mk is None else jnp.where(mk, w, 0.0) for w, mk in zip(ws, masks)]
            return [_dot(w.astype(BF16), v) for w, v in zip(ws, vs)]

        def live(carries):
            lowest = carries[0]
            for c in carries[1:]:
                lowest = jnp.minimum(lowest, c)
            return jnp.min(lowest) < SB_SKIP

        def q_group(blocks):
            streams = [(i, h) for i in blocks for h in range(len(heads))]
            qs = [q_ref[0, i * bq:(i + 1) * bq, heads[h]] for i, h in streams]
            carries = [jnp.zeros((bq, 1), F32) for _ in streams]
            accs = [jnp.zeros((bq, hd), F32) for _ in streams]

            kinds = [(d * bkn, d * bkn, True) for d in reversed(range(ndiag))] + [(0, -bkn, False)]
            items = [(s, rs, i * bq + off, masked) for rs, off, masked in kinds
                     for s, (i, _) in enumerate(streams) if masked or i > 0]

            def inside(ref):
                return [ref[0, k0:k0 + bkn, heads[streams[s][1]]] for s, _, k0, _ in items]

            visible = [(lax.broadcasted_iota(jnp.int32, (bq - rs, bkn), 1)
                        < lax.broadcasted_iota(jnp.int32, (bq - rs, bkn), 0)) if masked else None
                       for _, rs, _, masked in items]
            zs, sufs = scores([qs[s][rs:] for s, rs, _, _ in items], inside(k_ref), visible, m_new)
            newer = []
            for (s, rs, _, _), suf in zip(items, sufs):
                newer.append(carries[s][rs:])
                mass = suf[:, 0:1]
                if rs:
                    mass = jnp.concatenate([jnp.zeros((rs, 1), F32), mass], axis=0)
                carries[s] = carries[s] + mass
            for (s, rs, _, _), part in zip(items, weighted(zs, sufs, newer, inside(v_ref), visible)):
                if rs:
                    part = jnp.concatenate([jnp.zeros((rs, hd), F32), part], axis=0)
                accs[s] = accs[s] + part

            def sweep(ss, first, n_blocks, kref, vref, bk, m):
                def cond(st):
                    return jnp.logical_and(st[0] < n_blocks, live(st[1]))

                def body(st):
                    jj, cs, as_ = st
                    k0 = pl.multiple_of((n_blocks - 1 - jj) * bk, bk)
                    unmasked = [None] * len(ss)
                    zs, sufs = scores([qs[s] for s in ss], [kref[0, pl.ds(k0, bk), heads[streams[s][1]]] for s in ss],
                                      unmasked, m)
                    parts = weighted(zs, sufs, cs, [vref[0, pl.ds(k0, bk), heads[streams[s][1]]] for s in ss],
                                     unmasked)
                    return (jj + 1, tuple(c + suf[:, 0:1] for c, suf in zip(cs, sufs)),
                            tuple(a + p for a, p in zip(as_, parts)))

                _, cs, as_ = lax.while_loop(cond, body, (jnp.int32(first), tuple(carries[s] for s in ss),
                                                         tuple(accs[s] for s in ss)))
                for s, c, a in zip(ss, cs, as_):
                    carries[s], accs[s] = c, a

            for i in blocks:
                ss = [s for s, (bi, _) in enumerate(streams) if bi == i]
                if i * ndiag > 1:
                    sweep(ss, 1, i * ndiag, k_ref, v_ref, bkn, m_new)
                if past:
                    sweep(ss, 0, npast, kp_ref, vp_ref, bkp, m_past)
            for s, (i, h) in enumerate(streams):
                o_ref[0, i * bq:(i + 1) * bq, heads[h]] = accs[s].astype(BF16)

        for g0 in range(0, nq, SB_BLOCKS_PER_STAGE):
            q_group(list(range(g0, min(g0 + SB_BLOCKS_PER_STAGE, nq))))

    return kernel


def _sb_attention(q, k, v, k_past=None, v_past=None):
    B, T, _ = q.shape
    past = 0 if k_past is None else k_past.shape[1]
    bq = min(SB_Q_ROWS, T)
    bkn = min(SB_KEY_BLOCK, T)
    seq_spec = pl.BlockSpec((1, T, LANES), lambda b, h: (b, 0, h))
    in_specs = [seq_spec] * 3
    args = [q, k, v]
    if past:
        in_specs += [pl.BlockSpec((1, past, LANES), lambda b, h: (b, 0, h))] * 2
        args += [k_past, v_past]
    return pl.pallas_call(
        _make_attn_kernel(T, past, bq, bkn, SB_KEY_BLOCK),
        grid=(B, D_MODEL // LANES),
        in_specs=in_specs,
        out_specs=seq_spec,
        out_shape=jax.ShapeDtypeStruct((B, T, D_MODEL), BF16),
        compiler_params=_cparams(("parallel", "parallel")),
        name="sb_attn",
    )(*args)


def _gelu_tanh(x):
    return 0.5 * x * (1.0 + jnp.tanh(0.7978845608028654 * (x + 0.044715 * (x * x * x))))


def _make_s5_kernel(c, bb, lane_chunk):
    rows = c * bb
    tile_lanes = S5_GROUPS_PER_TILE * S5_STATE
    tile_cols = S5_GROUPS_PER_TILE * S5_GROUP

    def kernel(x_ref, sh_ref, sc_ref, g_ref, wre_ref, wim_ref, are_ref, aim_ref, cw_ref, d_ref,
               s0r_ref, s0i_ref, y_ref, sr_ref, si_ref, hm_ref, xr_ref, xi_ref):
        @pl.when(pl.program_id(1) == 0)
        def _():
            sr_ref[...] = s0r_ref[...]
            si_ref[...] = s0i_ref[...]

        xt = jnp.swapaxes(x_ref[...], 0, 1)
        hm = _norm_mod(xt, g_ref[...], sc_ref[...][None], sh_ref[...][None])
        hm_ref[...] = hm.reshape(rows, D_MODEL)
        for kt in range(S5_TILES):
            hb = hm_ref[:, kt * tile_cols:(kt + 1) * tile_cols].astype(BF16)
            xr_ref[:, kt * tile_lanes:(kt + 1) * tile_lanes] = _dot(hb, wre_ref[kt])
            xi_ref[:, kt * tile_lanes:(kt + 1) * tile_lanes] = _dot(hb, wim_ref[kt])

        for nt in range(S5_TILES):
            for lc in range(nt * tile_lanes // lane_chunk, (nt + 1) * tile_lanes // lane_chunk):
                ls = slice(lc * lane_chunk, (lc + 1) * lane_chunk)
                a_r = are_ref[:, ls]
                a_i = aim_ref[:, ls]
                s_r = sr_ref[:, ls]
                s_i = si_ref[:, ls]
                for t in range(c):
                    rs = slice(t * bb, (t + 1) * bb)
                    s_r, s_i = (a_r * s_r - a_i * s_i + xr_ref[rs, ls], a_r * s_i + a_i * s_r + xi_ref[rs, ls])
                    xr_ref[rs, ls] = s_r
                    xi_ref[rs, ls] = s_i
                sr_ref[:, ls] = s_r
                si_ref[:, ls] = s_i

            ls = slice(nt * tile_lanes, (nt + 1) * tile_lanes)
            cs = slice(nt * tile_cols, (nt + 1) * tile_cols)
            st = jnp.concatenate([xr_ref[:, ls].astype(BF16), xi_ref[:, ls].astype(BF16)], axis=1)
            yv = _dot(st, cw_ref[nt]) + d_ref[:, cs] * hm_ref[:, cs]
            y_ref[:, :, cs] = jnp.swapaxes(_gelu_tanh(yv).reshape(c, bb, tile_cols), 0, 1).astype(BF16)

    return kernel


def _s5_scan(x, sh, sc, g, wre, wim, a_re, a_im, cw, d, s0r, s0i):
    B, T, _ = x.shape
    bb = min(B, 16)
    c = min(32, T)
    x_spec = pl.BlockSpec((bb, c, D_MODEL), lambda b, t: (b, t, 0))
    b_spec = pl.BlockSpec((bb, D_MODEL), lambda b, t: (b, 0))
    s_spec = pl.BlockSpec((bb, S5_LANES), lambda b, t: (b, 0))
    tile_lanes = S5_GROUPS_PER_TILE * S5_STATE
    tile_cols = S5_GROUPS_PER_TILE * S5_GROUP
    return pl.pallas_call(
        _make_s5_kernel(c, bb, 512),
        grid=(B // bb, T // c),
        in_specs=[x_spec, b_spec, b_spec, _resident((1, D_MODEL)),
                  _resident((S5_TILES, tile_cols, tile_lanes)), _resident((S5_TILES, tile_cols, tile_lanes)),
                  _resident((1, S5_LANES)), _resident((1, S5_LANES)),
                  _resident((S5_TILES, 2 * tile_lanes, tile_cols)), _resident((1, D_MODEL)),
                  s_spec, s_spec],
        out_specs=[x_spec, s_spec, s_spec],
        out_shape=[jax.ShapeDtypeStruct((B, T, D_MODEL), BF16), jax.ShapeDtypeStruct((B, S5_LANES), F32),
                   jax.ShapeDtypeStruct((B, S5_LANES), F32)],
        scratch_shapes=[pltpu.VMEM((c * bb, D_MODEL), F32), pltpu.VMEM((c * bb, S5_LANES), F32),
                        pltpu.VMEM((c * bb, S5_LANES), F32)],
        compiler_params=_cparams(("parallel", "arbitrary")),
        name="s5_scan",
    )(x, sh, sc, g, wre, wim, a_re, a_im, cw, d, s0r, s0i)


def _s5_params(a_re, a_im, log_dt, b_re, b_im, c_re, c_im):
    dt = jnp.exp(log_dt.astype(F32))[:, None]
    ar, ai = a_re.astype(F32), a_im.astype(F32)
    mag = jnp.exp(ar * dt)
    abar_re, abar_im = mag * jnp.cos(ai * dt), mag * jnp.sin(ai * dt)
    den = ar * ar + ai * ai
    f_re = ((abar_re - 1.0) * ar + abar_im * ai) / den
    f_im = (abar_im * ar - (abar_re - 1.0) * ai) / den
    br, bi = b_re.astype(F32), b_im.astype(F32)
    bb_re = f_re[..., None] * br - f_im[..., None] * bi
    bb_im = f_re[..., None] * bi + f_im[..., None] * br
    gt = S5_GROUPS_PER_TILE
    eye = jnp.eye(gt, dtype=F32)

    def pack_in(w):
        w = w.reshape(S5_TILES, gt, S5_STATE, S5_GROUP)
        return jnp.einsum('kgpn,gh->kgnhp', w, eye).reshape(S5_TILES, gt * S5_GROUP, gt * S5_STATE).astype(BF16)

    def pack_out(w):
        w = w.reshape(S5_TILES, gt, S5_GROUP, S5_STATE)
        return jnp.einsum('kgnp,gh->kgphn', w, eye).reshape(S5_TILES, gt * S5_STATE, gt * S5_GROUP)

    cw = jnp.concatenate([pack_out(c_re.astype(F32)), -pack_out(c_im.astype(F32))], axis=1).astype(BF16)
    return (pack_in(bb_re), pack_in(bb_im), abar_re.reshape(1, S5_LANES), abar_im.reshape(1, S5_LANES), cw)


def _make_gdn_kernel(T, c, hb, cu):
    n = T // c
    span = cu * c
    dh = GDN_HEAD_DIM
    ng = c // SUBLANES
    lanes = tuple(slice(j * dh, (j + 1) * dh) for j in range(hb))
    blk = min(GDN_SOLVE_BLOCK, c)

    def kernel(q_ref, k_ref, v_ref, z_ref, ab_ref, alog_ref, dtb_ref, ng_ref, s0_ref, o_ref, s_ref,
               sol_ref, wq_scr, qkd_ref, kd_ref, gl_ref):
        h0 = pl.program_id(1) * hb
        s_ref[0] = s0_ref[0]

        r = lax.broadcasted_iota(jnp.int32, (c, c), 0)
        col = lax.broadcasted_iota(jnp.int32, (c, c), 1)
        incl = r >= col
        strict = r > col
        eye = r == col
        l_incl = incl.astype(F32)
        lane = lax.broadcasted_iota(jnp.int32, (1, LANES), 1)
        neg_decay_rate = -jnp.exp(alog_ref[...])

        def pick(sel, m):
            return jnp.sum(jnp.where(sel, m, 0.0), axis=-1, keepdims=True)

        def row0(ci):
            return ci * span if isinstance(ci, int) else pl.multiple_of(ci * span, span)

        def solve(ci, tick):
            t0 = row0(ci)
            pairs = [(u, j) for u in range(cu) for j in range(hb)]
            hs = range(len(pairs))
            ab = ab_ref[0, pl.ds(t0, span), :]
            g_all = neg_decay_rate * _softplus(ab + dtb_ref[...])
            gcum_all = [_dot_f32(l_incl, g_all[u * c:(u + 1) * c]) for u in range(cu)]
            beta_all = _sigmoid(ab)

            def rows(u):
                return slice(u * c, (u + 1) * c)

            def chunk_of(ref, u, j):
                return ref[0, pl.ds(t0 + u * c, c) if isinstance(t0, int) else pl.ds(pl.multiple_of(t0 + u * c, c), c),
                           lanes[j]]

            q = [chunk_of(q_ref, u, j) for u, j in pairs]
            k = [chunk_of(k_ref, u, j) for u, j in pairs]
            v = [chunk_of(v_ref, u, j) for u, j in pairs]
            gcum = [pick(lane == h0 + j, gcum_all[u]) for u, j in pairs]
            beta = [pick(lane == h0 + j + GDN_HEADS, beta_all[rows(u)]) for u, j in pairs]
            gcum_row = [jnp.sum(jnp.where(eye, g, 0.0), axis=0, keepdims=True) for g in gcum]
            decay = [jnp.where(incl, jnp.exp(jnp.where(incl, g - gr, 0.0)), 0.0) for g, gr in zip(gcum, gcum_row)]
            kb = [x.astype(BF16) for x in k]
            kk = [_dot_nt(x, x) for x in kb]
            qk = [_dot_nt(x.astype(BF16), y) for x, y in zip(q, kb)]
            a = [jnp.where(strict, b * m * d, 0.0) for b, m, d in zip(beta, kk, decay)]
            tick()

            inv = [[jnp.where(eye[g * SUBLANES:(g + 1) * SUBLANES], 1.0, 0.0).astype(F32) for g in range(ng)]
                   for _ in hs]
            a_g = [[m[g * SUBLANES:(g + 1) * SUBLANES] for g in range(ng)] for m in a]
            gpb = blk // SUBLANES
            for step in range(blk - 1):
                if step % 4 == 3:
                    tick()
                for b in range(c // blk):
                    g0, jr = divmod(b * blk + step, SUBLANES)
                    for j in hs:
                        row = inv[j][g0][jr:jr + 1, :]
                        for g in range(g0, (b + 1) * gpb):
                            inv[j][g] = inv[j][g] - a_g[j][g][:, b * blk + step:b * blk + step + 1] * row
            inv = [jnp.concatenate(m, axis=0) for m in inv]
            size = blk
            while size < c:
                sh = size.bit_length() - 1
                off_mask = jnp.logical_and(r >> (sh + 1) == col >> (sh + 1), r >> sh != col >> sh)
                off = [jnp.where(off_mask, m, 0.0).astype(BF16) for m in a]
                inv_b = [m.astype(BF16) for m in inv]
                t = [_dot(off[j], inv_b[j]).astype(BF16) for j in hs]
                tick()
                inv = [inv[j] - _dot(inv_b[j], t[j]) for j in hs]
                tick()
                size *= 2

            eg = [jnp.exp(g) for g in gcum]
            inv_b = [m.astype(BF16) for m in inv]
            rhs = [jnp.concatenate([beta[j] * v[j], (beta[j] * eg[j]) * k[j]], axis=1) for j in hs]
            rhs_hi = [m.astype(BF16) for m in rhs]
            rhs_lo = [(m - mh.astype(F32)).astype(BF16) for m, mh in zip(rhs, rhs_hi)]
            sol = [_dot(inv_b[j], rhs_hi[j]) + _dot(inv_b[j], rhs_lo[j]) for j in hs]
            wq = [jnp.concatenate([sol[j][:, dh:2 * dh].astype(BF16), (q[j] * eg[j]).astype(BF16)], axis=0) for j in hs]
            qkd = [(qk[j] * decay[j]).astype(BF16) for j in hs]
            g_last = [gcum[j][c - 1:c, :] for j in hs]
            kd = [(k[j] * jnp.exp(g_last[j] - gcum[j])).astype(BF16) for j in hs]
            return sol, wq, qkd, kd, [jnp.exp(g) for g in g_last]

        def park(sol, wq, qkd, kd, decay_last):
            for i in range(cu * hb):
                sol_ref[i] = sol[i]
                wq_scr[i] = wq[i]
                qkd_ref[i] = qkd[i]
                kd_ref[i] = kd[i]
                gl_ref[i] = jnp.broadcast_to(decay_last[i], (SUBLANES, LANES))

        def recur(ci):
            t0 = row0(ci)
            for u in range(cu):
                tu = t0 + u * c if isinstance(t0, int) else pl.multiple_of(t0 + u * c, c)
                ids = [u * hb + j for j in range(hb)]
                s = [s_ref[0, j] for j in range(hb)]
                sb = [x.astype(BF16) for x in s]
                ws = [_dot(wq_scr[i], sb[j]) for j, i in enumerate(ids)]
                yield
                vb = [(sol_ref[i, :, 0:dh] - ws[j][0:c]).astype(BF16) for j, i in enumerate(ids)]
                o = [ws[j][c:2 * c] + _dot(qkd_ref[i], vb[j]) for j, i in enumerate(ids)]
                for j, i in enumerate(ids):
                    s_ref[0, j] = s[j] * gl_ref[i, 0:1, 0:1] + lax.dot_general(
                        kd_ref[i], vb[j], (((0,), (0,)), ((), ())), preferred_element_type=F32)
                for j, ls in enumerate(lanes):
                    y = _rms(o[j]) * ng_ref[...] * _silu(z_ref[0, pl.ds(tu, c), ls])
                    o_ref[0, pl.ds(tu, c), ls] = y.astype(BF16)
                yield

        park(*solve(0, lambda: None))

        def body(ci, carry):
            rec = recur(ci - 1)
            solved = solve(ci, lambda: next(rec, None))
            for _ in rec:
                pass
            park(*solved)
            return carry

        steps = n // cu
        if steps > 1:
            lax.fori_loop(1, steps, body, 0)
        for _ in recur(steps - 1):
            pass

    return kernel


def _gdn_core(qkv, z, ab, alog_row, dtb_row, norm_g, s0):
    B, T, _ = qkv.shape
    c = min(GDN_CHUNK, T)
    H, dh = GDN_HEADS, GDN_HEAD_DIM
    hb = GDN_HEADS_PER_STEP
    G = H // hb

    def seq(off):
        return pl.BlockSpec((1, T, hb * dh), lambda b, h: (b, 0, off + h))

    st_spec = pl.BlockSpec((1, hb, dh, dh), lambda b, h: (b, h, 0, 0))
    cu = GDN_CHUNKS_PER_STEP if (T // c) % GDN_CHUNKS_PER_STEP == 0 else 1
    return pl.pallas_call(
        _make_gdn_kernel(T, c, hb, cu),
        grid=(B, G),
        in_specs=[seq(0), seq(G), seq(2 * G), seq(0),
                  pl.BlockSpec((1, T, LANES), lambda b, h: (b, 0, 0)),
                  _resident((1, LANES)), _resident((1, LANES)), _resident((1, dh)), st_spec],
        out_specs=[seq(0), st_spec],
        out_shape=[jax.ShapeDtypeStruct((B, T, D_MODEL), BF16), jax.ShapeDtypeStruct((B, H, dh, dh), F32)],
        scratch_shapes=[pltpu.VMEM((cu * hb, c, 2 * dh), F32), pltpu.VMEM((cu * hb, 2 * c, dh), BF16),
                        pltpu.VMEM((cu * hb, c, c), BF16), pltpu.VMEM((cu * hb, c, dh), BF16),
                        pltpu.VMEM((cu * hb, SUBLANES, LANES), F32)],
        compiler_params=_cparams(("parallel", "parallel")),
        name="gdn_core",
    )(qkv, qkv, qkv, z, ab, alog_row, dtb_row, norm_g, s0)


def _make_conv_kernel(T, tq, rsub, lchunk):
    halo = CNV_HALO
    first_hist = halo - (CNV_WIDTH - 1)

    shifted_rows = tq + halo - SUBLANES

    def kernel(u_ref, cb_ref, w_ref, b_ref, lg_ref, lb_ref, y_ref, us_ref, acc_ref):
        t = pl.program_id(1)
        t0 = pl.multiple_of(t * tq, tq)

        @pl.when(t == 0)
        def _():
            us_ref[0, 0:halo, :] = jnp.zeros((halo, D_MODEL), F32)
            us_ref[0, first_hist:halo, :] = cb_ref[0]

        @pl.when(t > 0)
        def _():
            us_ref[0, 0:halo, :] = u_ref[0, pl.ds(t0 - halo, halo), :]

        us_ref[0, halo:halo + tq, :] = u_ref[0, pl.ds(t0, tq), :]
        for b in range(1, SUBLANES):
            us_ref[b, 0:shifted_rows, :] = us_ref[0, b:b + shifted_rows, :]

        for lc in range(D_MODEL // lchunk):
            ls = slice(lc * lchunk, (lc + 1) * lchunk)

            def rows(ri, carry):
                r0 = pl.multiple_of(ri * rsub, rsub)
                acc = b_ref[:, ls]
                for i in range(CNV_WIDTH):
                    a, b = divmod(first_hist + i, SUBLANES)
                    acc = acc + w_ref[i:i + 1, ls] * us_ref[b, pl.ds(r0 + a * SUBLANES, rsub), ls]
                acc_ref[pl.ds(r0, rsub), ls] = acc
                return carry

            lax.fori_loop(0, tq // rsub, rows, 0)

        y = acc_ref[...]
        yc = y - jnp.mean(y, axis=-1, keepdims=True)
        var = jnp.mean(yc * yc, axis=-1, keepdims=True)
        y_ref[0] = _silu(yc * lax.rsqrt(var + EPS) * lg_ref[...] + lb_ref[...]).astype(BF16)

    return kernel


def _conv_core(u, cbuf, dw_w, dw_b, ln_g, ln_b):
    B, T, _ = u.shape
    tq = min(256, T)
    rsub = min(32, tq)
    row = _resident((1, D_MODEL))
    return pl.pallas_call(
        _make_conv_kernel(T, tq, rsub, 512),
        grid=(B, T // tq),
        in_specs=[pl.BlockSpec((1, T, D_MODEL), lambda b, t: (b, 0, 0)),
                  pl.BlockSpec((1, CNV_WIDTH - 1, D_MODEL), lambda b, t: (b, 0, 0)),
                  _resident((CNV_WIDTH, D_MODEL)), row, row, row],
        out_specs=pl.BlockSpec((1, tq, D_MODEL), lambda b, t: (b, t, 0)),
        out_shape=jax.ShapeDtypeStruct((B, T, D_MODEL), BF16),
        scratch_shapes=[pltpu.VMEM((SUBLANES, tq + CNV_HALO, D_MODEL), F32), pltpu.VMEM((tq, D_MODEL), F32)],
        compiler_params=_cparams(("parallel", "arbitrary")),
        name="cnv_core",
    )(u, cbuf, dw_w, dw_b, ln_g, ln_b)


def _row(v):
    return v.astype(F32).reshape(1, -1)


def _pad_lanes(v):
    return jnp.pad(v.astype(F32), (0, LANES - v.shape[0])).reshape(1, LANES)


def _prep_weights(p):
    w = {}
    w['w1'] = [p['ffn_w1'][i].astype(BF16) for i in range(DEPTH)]
    w['w2'] = [p['ffn_w2'][i].astype(BF16) for i in range(DEPTH)]
    w['sb_qkv'] = p['sb_w_qkv'][0].astype(BF16)
    w['sb_o'] = p['sb_w_o'][0].astype(BF16)
    w['s5'] = _s5_params(p['s5_a_re'][0], p['s5_a_im'][0], p['s5_log_dt'][0], p['s5_b_re'][0], p['s5_b_im'][0],
                         p['s5_c_re'][0], p['s5_c_im'][0])
    w['s5_glu'] = p['s5_w_glu'][0].astype(BF16)
    w_in = p['gdn_w_in'][0]
    w['gdn_in'] = w_in[:, :4 * D_MODEL].astype(BF16)
    w['gdn_ab'] = jnp.pad(w_in[:, 4 * D_MODEL:], ((0, 0), (0, LANES - 2 * GDN_HEADS))).astype(BF16)
    w['gdn_o'] = p['gdn_w_o'][0].astype(BF16)
    w['pw1'] = p['cnv_w_pw1'][0].astype(BF16)
    w['pw2'] = p['cnv_w_pw2'][0].astype(BF16)
    return w


def _trunk(x, mod, sb_k, sb_v, s5_re, s5_im, gdn_s, gdn_cb, cnv_cb, p, w):
    B, T, _ = x.shape
    out = {}
    g_out = _row(p['norm_out_g'])

    def tail(i, x, y, w_pre, glu=False):
        return _tail(x, y, mod[i], w_pre, _row(p['norm_ffn_g'][i]), w['w1'][i], w['w2'][i], g_out,
                     glu=glu, final=(i == DEPTH - 1))

    q, k, v, kb, vb = _qkv_proj(x, mod[0], _row(p['norm_mix_g'][0]), w['sb_qkv'])
    if sb_k is None:
        o = _sb_attention(q, kb, vb)
    else:
        P = sb_k.shape[1]
        o = _sb_attention(q, kb, vb, sb_k.reshape(B, P, D_MODEL).astype(BF16),
                          sb_v.reshape(B, P, D_MODEL).astype(BF16))
    out['sb_k'] = k.reshape(1, B, T, SB_HEADS, SB_HEAD_DIM)
    out['sb_v'] = v.reshape(1, B, T, SB_HEADS, SB_HEAD_DIM)
    x = tail(0, x, o, w['sb_o'])

    wre, wim, a_re, a_im, cw = w['s5']
    yt, s_re, s_im = _s5_scan(x, mod[1][:, 0, :], mod[1][:, 1, :], _row(p['norm_mix_g'][1]),
                              wre, wim, a_re, a_im, cw, _row(p['s5_d'][0]),
                              s5_re.reshape(B, S5_LANES), s5_im.reshape(B, S5_LANES))
    out['s5_re'] = s_re.reshape(1, B, S5_GROUPS, S5_STATE)
    out['s5_im'] = s_im.reshape(1, B, S5_GROUPS, S5_STATE)
    x = tail(1, x, yt, w['s5_glu'], glu=True)

    qkv, z, ab, hist = _gdn_in_proj(x, mod[2], _row(p['norm_mix_g'][2]), w['gdn_in'], w['gdn_ab'], gdn_cb,
                                    p['gdn_conv_w'][0].astype(F32))
    o, s_new = _gdn_core(qkv, z, ab, _pad_lanes(p['gdn_a_log'][0]), _pad_lanes(p['gdn_dt_bias'][0]),
                         _row(p['gdn_norm_g'][0]), gdn_s)
    out['gdn'] = s_new[None]
    out['gdn_conv'] = hist[None, :, SUBLANES - (GDN_CONV - 1):]
    x = tail(2, x, o, w['gdn_o'])

    u = _pw1_proj(x, mod[3], _row(p['norm_mix_g'][3]), w['pw1'])
    ya = _conv_core(u, cnv_cb, p['cnv_dw_w'][0].astype(F32), _row(p['cnv_dw_b'][0]), _row(p['cnv_ln_g'][0]),
                    _row(p['cnv_ln_b'][0]))
    hist = jnp.concatenate([cnv_cb, u], axis=1) if T < CNV_WIDTH - 1 else u
    out['conv'] = hist[None, :, hist.shape[1] - (CNV_WIDTH - 1):]
    x = tail(3, x, ya, w['pw2'])
    return x, out


def kernel(x_prompt, x_sample, c_prompt, c_sample, cache_sb_k, cache_sb_v, state_s5_re, state_s5_im, state_gdn, state_gdn_conv, state_conv, ada_w, ada_b, norm_mix_g, norm_ffn_g, norm_out_g, ffn_w1, ffn_w2, sb_w_qkv, sb_w_o, s5_a_re, s5_a_im, s5_log_dt, s5_b_re, s5_b_im, s5_c_re, s5_c_im, s5_d, s5_w_glu, gdn_w_in, gdn_conv_w, gdn_a_log, gdn_dt_bias, gdn_norm_g, gdn_w_o, cnv_w_pw1, cnv_dw_w, cnv_dw_b, cnv_ln_g, cnv_ln_b, cnv_w_pw2):
    p = dict(norm_mix_g=norm_mix_g, norm_ffn_g=norm_ffn_g, norm_out_g=norm_out_g, ffn_w1=ffn_w1, ffn_w2=ffn_w2,
             sb_w_qkv=sb_w_qkv, sb_w_o=sb_w_o, s5_a_re=s5_a_re, s5_a_im=s5_a_im, s5_log_dt=s5_log_dt,
             s5_b_re=s5_b_re, s5_b_im=s5_b_im, s5_c_re=s5_c_re, s5_c_im=s5_c_im, s5_d=s5_d, s5_w_glu=s5_w_glu,
             gdn_w_in=gdn_w_in, gdn_conv_w=gdn_conv_w, gdn_a_log=gdn_a_log, gdn_dt_bias=gdn_dt_bias,
             gdn_norm_g=gdn_norm_g, gdn_w_o=gdn_w_o, cnv_w_pw1=cnv_w_pw1, cnv_dw_w=cnv_dw_w, cnv_dw_b=cnv_dw_b,
             cnv_ln_g=cnv_ln_g, cnv_ln_b=cnv_ln_b, cnv_w_pw2=cnv_w_pw2)
    w = _prep_weights(p)
    bp, bs = x_prompt.shape[0], x_sample.shape[0]
    mod = _ada(jnp.concatenate([c_prompt, c_sample], axis=0), ada_w, ada_b)
    mod = mod.reshape(DEPTH, bp + bs, 6, D_MODEL)
    mod_p = [mod[i, :bp] for i in range(DEPTH)]
    mod_s = [mod[i, bp:] for i in range(DEPTH)]

    yp, op = _trunk(
        x_prompt, mod_p, None, None,
        jnp.zeros((bp, S5_GROUPS, S5_STATE), F32), jnp.zeros((bp, S5_GROUPS, S5_STATE), F32),
        jnp.zeros((bp, GDN_HEADS, GDN_HEAD_DIM, GDN_HEAD_DIM), F32),
        jnp.zeros((bp, GDN_CONV - 1, 3 * D_MODEL), F32), jnp.zeros((bp, CNV_WIDTH - 1, D_MODEL), F32), p, w)
    ys, os_ = _trunk(
        x_sample, mod_s, cache_sb_k[0], cache_sb_v[0], state_s5_re[0], state_s5_im[0],
        state_gdn[0], state_gdn_conv[0], state_conv[0], p, w)
    return (yp, ys, op['sb_k'], op['sb_v'], os_['sb_k'], os_['sb_v'],
            op['s5_re'], op['s5_im'], os_['s5_re'], os_['s5_im'],
            op['gdn'], op['gdn_conv'], os_['gdn'], os_['gdn_conv'],
            op['conv'], os_['conv'])
```
